```python
import math
import jax, jax.numpy as jnp
from jax import lax
import numpy as np

D_MODEL = 2048
BATCH = 16
SEQ = 256
DEPTH = 2
DEC_BATCH = 4
DEC_SEQ = 2048
PAST_LEN = 256

GRID_W = 64
EPS = 1e-6
SSD_HEAD_DIM = 64
SSD_INNER = D_MODEL
SSD_HEADS = SSD_INNER // SSD_HEAD_DIM
SSD_GROUPS = 4
SSD_STATE = 128
SSD_CONV = 3
SSD_CHUNK = 128
SSD_CONV_DIM = SSD_INNER + 2 * SSD_GROUPS * SSD_STATE
ATT_HEAD_DIM = 128
ATT_HEADS = D_MODEL // ATT_HEAD_DIM
ATT_KV_HEADS = ATT_HEADS // 4
ATT_BLOCK = 128
ROPE_THETA = 10000.0
RET_KEY_DIM = 128
RET_VAL_DIM = 256
RET_HEADS = D_MODEL // RET_VAL_DIM
RET_INNER = RET_HEADS * RET_VAL_DIM
RET_CHUNK = 128
N_BRANCHES = 3
N_EXPERTS = 16
EXPERT_FF = D_MODEL
CAPACITY_FACTOR = 2

IN_SIZES = (SSD_INNER, SSD_CONV_DIM, 2 * SSD_HEADS,
            ATT_HEADS * ATT_HEAD_DIM, ATT_KV_HEADS * ATT_HEAD_DIM, ATT_KV_HEADS * ATT_HEAD_DIM,
            RET_HEADS * RET_KEY_DIM, RET_HEADS * RET_KEY_DIM, RET_INNER, RET_INNER,
            N_BRANCHES * D_MODEL)
IN_WIDTH = sum(IN_SIZES)
IN_SPLIT_POINTS = tuple(int(v) for v in np.cumsum(IN_SIZES)[:-1])

kernel_name = 'hybrid_diffusion_ssd_gqa_retention_ec'

F32 = jnp.float32


def rmsnorm(x, g):
    xf = x.astype(F32)
    y = xf * lax.rsqrt(jnp.mean(xf * xf, axis=-1, keepdims=True) + EPS)
    return (y * g.astype(F32)).astype(x.dtype)


def flip(t):
    return jnp.flip(t, axis=1)


def axial_rope_tables(n_tokens, dim):
    rows_n = n_tokens // GRID_W
    row = jnp.repeat(jnp.arange(rows_n, dtype=F32), GRID_W)
    col = jnp.tile(jnp.arange(GRID_W, dtype=F32), rows_n)
    n_freq = dim // 4
    inv = ROPE_THETA ** (-jnp.arange(n_freq, dtype=F32) / n_freq)
    ang = jnp.stack([row[:, None] * inv, col[:, None] * inv], axis=1)
    return jnp.cos(ang), jnp.sin(ang)


def apply_axial_rope(x, cos, sin):
    b, l, h, d = x.shape
    xf = x.astype(F32).reshape(b, l, h, 2, 2, d // 4)
    x1, x2 = xf[..., 0, :], xf[..., 1, :]
    c, s = cos[None, :, None], sin[None, :, None]
    out = jnp.stack([x1 * c - x2 * s, x2 * c + x1 * s], axis=-2)
    return out.reshape(b, l, h, d).astype(x.dtype)


def depthwise_conv_centred(x, w, bias):
    k = w.shape[0]
    out = lax.conv_general_dilated(x, w[:, None, :].astype(x.dtype), window_strides=(1,),
                                   padding=[(k // 2, k // 2)],
                                   dimension_numbers=('NWC', 'WIO', 'NWC'),
                                   feature_group_count=x.shape[-1])
    return out + bias


def ssd_chunked_scan(x, dt, a_log, bm, cm, h0):
    b, l, nh, hp = x.shape
    g, n = bm.shape[-2:]
    r = nh // g
    q = SSD_CHUNK
    nc = l // q
    a = dt * (-jnp.exp(a_log.astype(F32)))
    xc = x.astype(F32).reshape(b, nc, q, g, r, hp)
    dtc = dt.reshape(b, nc, q, g, r)
    acum = jnp.cumsum(a.reshape(b, nc, q, g, r), axis=2)
    bc = bm.astype(F32).reshape(b, nc, q, g, n)
    cc = cm.astype(F32).reshape(b, nc, q, g, n)
    causal = jnp.tril(jnp.ones((q, q), bool))[None, None, :, :, None, None]
    seg = acum[:, :, :, None] - acum[:, :, None, :]
    decay = jnp.exp(jnp.where(causal, seg, -jnp.inf))
    cb = jnp.einsum('bcign,bcjgn->bcijg', cc, bc)
    w = cb[..., None] * decay * dtc[:, :, None]
    y_intra = jnp.einsum('bcijgr,bcjgrp->bcigrp', w, xc)
    to_end = jnp.exp(acum[:, :, -1:] - acum) * dtc
    states = jnp.einsum('bcjgn,bcjgr,bcjgrp->bcgrpn', bc, to_end, xc)
    chunk_decay = jnp.exp(acum[:, :, -1])

    def step(h, inp):
        dec, st = inp
        return dec[..., None, None] * h + st, h

    h_final, h_prev = lax.scan(step, h0.astype(F32).reshape(b, g, r, hp, n),
                               (jnp.moveaxis(chunk_decay, 1, 0), jnp.moveaxis(states, 1, 0)))
    h_prev = jnp.moveaxis(h_prev, 0, 1)
    y_inter = jnp.einsum('bcign,bcgrpn,bcigr->bcigrp', cc, h_prev, jnp.exp(acum))
    return (y_intra + y_inter).reshape(b, l, nh, hp), h_final.reshape(b, nh, hp, n)


def retention_chunked(q, k, v, log_gamma, s0, inclusive):
    b, l, nh, dk = q.shape
    dv = v.shape[-1]
    cl = RET_CHUNK
    nc = l // cl
    qc = q.astype(F32).reshape(b, nc, cl, nh, dk)
    kc = k.astype(F32).reshape(b, nc, cl, nh, dk)
    vc = v.astype(F32).reshape(b, nc, cl, nh, dv)
    pos = jnp.arange(cl, dtype=F32)
    diff = pos[:, None] - pos[None, :]
    mask = (diff >= 0) if inclusive else (diff > 0)
    dmat = jnp.exp(jnp.where(mask[..., None], diff[..., None] * log_gamma, -jnp.inf))
    scores = jnp.einsum('bcihk,bcjhk->bcijh', qc, kc) * dmat
    intra = jnp.einsum('bcijh,bcjhv->bcihv', scores, vc)
    k_to_end = jnp.exp((cl - 1 - pos)[:, None] * log_gamma)
    states = jnp.einsum('bcjhk,jh,bcjhv->bchkv', kc, k_to_end, vc)
    chunk_decay = jnp.exp(cl * log_gamma)

    def step(s, st):
        return chunk_decay[:, None, None] * s + st, s

    s_final, s_prev = lax.scan(step, s0.astype(F32), jnp.moveaxis(states, 1, 0))
    s_prev = jnp.moveaxis(s_prev, 0, 1)
    q_from_start = jnp.exp((pos + 1)[:, None] * log_gamma)
    inter = jnp.einsum('bcihk,ih,bchkv->bcihv', qc, q_from_start, s_prev)
    return (intra + inter).reshape(b, l, nh, dv), s_final


def block_attention(q, k, v):
    b, l, hq, d = q.shape
    hkv = k.shape[2]
    rep = hq // hkv
    nb = l // ATT_BLOCK
    qb = jnp.moveaxis(q.reshape(b, nb, ATT_BLOCK, hkv, rep, d), 1, 0)
    kf = k.astype(F32)
    vf = v.astype(F32)
    scale = d ** -0.5

    def one_block(qblk):
        s = jnp.einsum('bqgrd,bkgd->bgrqk', qblk.astype(F32), kf) * scale
        p = jax.nn.softmax(s, axis=-1)
        return jnp.einsum('bgrqk,bkgd->bqgrd', p, vf)

    o = lax.map(one_block, qb)
    return jnp.moveaxis(o, 0, 1).reshape(b, l, hq * d).astype(q.dtype)


def token_mixer(hn, p, rope, ctx):
    b, l, _ = hn.shape
    latent = ctx is not None
    proj = hn @ p['w_in']
    z, xbc, dt_raw, aq, ak, av, rq, rk, rv, rg, gl = jnp.split(proj, IN_SPLIT_POINTS, axis=-1)

    xbc = jax.nn.silu(depthwise_conv_centred(xbc, p['ssd_conv_w'], p['ssd_conv_b']))
    xs, bm, cm = jnp.split(xbc, [SSD_INNER, SSD_INNER + SSD_GROUPS * SSD_STATE], axis=-1)
    xs = xs.reshape(b, l, SSD_HEADS, SSD_HEAD_DIM)
    bm = bm.reshape(b, l, SSD_GROUPS, SSD_STATE)
    cm = cm.reshape(b, l, SSD_GROUPS, SSD_STATE)
    dt = jax.nn.softplus(dt_raw.reshape(b, l, 2, SSD_HEADS).astype(F32) + p['ssd_dt_bias'].astype(F32))
    if latent:
        h0f, h0b = ctx[2], ctx[3]
    else:
        h0f = jnp.zeros((b, SSD_HEADS, SSD_HEAD_DIM, SSD_STATE), F32)
        h0b = h0f
    yf, hf = ssd_chunked_scan(xs, dt[:, :, 0], p['ssd_a_log'][0], bm, cm, h0f)
    yb, hb = ssd_chunked_scan(flip(xs), flip(dt[:, :, 1]), p['ssd_a_log'][1], flip(bm), flip(cm), h0b)
    y = yf + flip(yb) + p['ssd_d'].astype(F32)[:, None] * xs.astype(F32)
    y_ssd = rmsnorm(y.reshape(b, l, SSD_INNER) * jax.nn.silu(z.astype(F32)), p['ssd_norm_g']).astype(hn.dtype)

    q = rmsnorm(aq.reshape(b, l, ATT_HEADS, ATT_HEAD_DIM), p['q_norm_g'])
    k = rmsnorm(ak.reshape(b, l, ATT_KV_HEADS, ATT_HEAD_DIM), p['k_norm_g'])
    v = av.reshape(b, l, ATT_KV_HEADS, ATT_HEAD_DIM)
    if latent:
        q = apply_axial_rope(q, rope[0], rope[1])
        k_lat = apply_axial_rope(k, rope[0], rope[1])
        k_all = jnp.concatenate([jnp.swapaxes(ctx[0], 1, 2).astype(hn.dtype), k_lat], axis=1)
        v_all = jnp.concatenate([jnp.swapaxes(ctx[1], 1, 2).astype(hn.dtype), v], axis=1)
    else:
        k_all, v_all = k, v
    y_att = block_attention(q, k_all, v_all)

    rq = rq.reshape(b, l, RET_HEADS, RET_KEY_DIM) * (RET_KEY_DIM ** -0.5)
    rk = rk.reshape(b, l, RET_HEADS, RET_KEY_DIM)
    rv = rv.reshape(b, l, RET_HEADS, RET_VAL_DIM)
    if latent:
        rq = apply_axial_rope(rq, rope[0], rope[1])
        rk = apply_axial_rope(rk, rope[0], rope[1])
        s0f, s0b = ctx[4], ctx[5]
    else:
        s0f = jnp.zeros((b, RET_HEADS, RET_KEY_DIM, RET_VAL_DIM), F32)
        s0b = s0f
    log_gamma = jax.nn.log_sigmoid(p['ret_decay'].astype(F32))
    of, sf = retention_chunked(rq, rk, rv, log_gamma[0], s0f, True)
    ob, sb = retention_chunked(flip(rq), flip(rk), flip(rv), log_gamma[1], s0b, False)
    o = rmsnorm(of + flip(ob), p['ret_norm_g'].reshape(RET_HEADS, RET_VAL_DIM))
    y_ret = (jax.nn.silu(rg.astype(F32)) * o.reshape(b, l, RET_INNER)).astype(hn.dtype)

    branches = jnp.stack([y_ssd, y_att, y_ret], axis=2)
    proj_b = jnp.einsum('blnw,nwd->blnd', branches, p['w_branch'])
    gates = jax.nn.sigmoid(gl.reshape(b, l, N_BRANCHES, D_MODEL))
    out = jnp.sum(gates * proj_b, axis=2) @ p['w_out']
    if latent:
        return out, None
    dt_out = hn.dtype
    new_ctx = (jnp.swapaxes(k, 1, 2), jnp.swapaxes(v, 1, 2), hf.astype(dt_out), hb.astype(dt_out),
               sf.astype(dt_out), sb.astype(dt_out))
    return out, new_ctx


def expert_choice_ffn(h, w_router, w_e1, w_e3, w_e2):
    b, l, d = h.shape
    n = b * l
    xf = h.reshape(n, d)
    cap = CAPACITY_FACTOR * n // N_EXPERTS
    aff = jax.nn.softmax((xf @ w_router).astype(F32), axis=-1)
    gate, idx = lax.top_k(aff.T, cap)
    xe = xf[idx]
    hid = jax.nn.silu(jnp.einsum('ecd,edf->ecf', xe, w_e1)) * jnp.einsum('ecd,edf->ecf', xe, w_e3)
    ye = jnp.einsum('ecf,efd->ecd', hid, w_e2) * gate[..., None].astype(h.dtype)
    out = jnp.zeros_like(xf).at[idx.reshape(-1)].add(ye.reshape(-1, d))
    return out.reshape(b, l, d)


def trunk_layer(x, cond, p, rope, ctx):
    mod = (jax.nn.silu(cond) @ p['w_ada'] + p['b_ada'])[:, None, :]
    sh1, sc1, g1, sh2, sc2, g2 = jnp.split(mod, 6, axis=-1)
    h = rmsnorm(x, p['norm1_g']) * (1 + sc1) + sh1
    mix, new_ctx = token_mixer(h, p, rope, ctx)
    x = x + g1 * mix
    h = rmsnorm(x, p['norm2_g']) * (1 + sc2) + sh2
    x = x + g2 * expert_choice_ffn(h, p['w_router'], p['w_e1'], p['w_e3'], p['w_e2'])
    return x, new_ctx


def setup_inputs(seed: int = 0) -> dict:
    key = jax.random.key(seed)
    ks = iter(jax.random.split(key, 48))

    def nrm(shape, scale):
        return jax.random.normal(next(ks), shape, F32) * scale

    dsc = D_MODEL ** -0.5
    x_prompt = nrm((BATCH, SEQ, D_MODEL), 1.0)
    x_sample = nrm((DEC_BATCH, DEC_SEQ, D_MODEL), 1.0)
    cache_attn_k = nrm((DEC_BATCH, DEPTH, ATT_KV_HEADS, PAST_LEN, ATT_HEAD_DIM), 1.0)
    cache_attn_v = nrm((DEC_BATCH, DEPTH, ATT_KV_HEADS, PAST_LEN, ATT_HEAD_DIM), 1.0)
    state_ssd_fwd = nrm((DEC_BATCH, DEPTH, SSD_HEADS, SSD_HEAD_DIM, SSD_STATE), 0.1)
    state_ssd_bwd = nrm((DEC_BATCH, DEPTH, SSD_HEADS, SSD_HEAD_DIM, SSD_STATE), 0.1)
    state_ret_fwd = nrm((DEC_BATCH, DEPTH, RET_HEADS, RET_KEY_DIM, RET_VAL_DIM), 0.1)
    state_ret_bwd = nrm((DEC_BATCH, DEPTH, RET_HEADS, RET_KEY_DIM, RET_VAL_DIM), 0.1)
    c = nrm((DEC_BATCH, D_MODEL), 1.0)
    c_ctx = nrm((D_MODEL,), 1.0)
    norm1_g = 1.0 + nrm((DEPTH, D_MODEL), 0.02)
    norm2_g = 1.0 + nrm((DEPTH, D_MODEL), 0.02)
    w_ada = nrm((DEPTH, D_MODEL, 6 * D_MODEL), dsc)
    b_ada = nrm((DEPTH, 6 * D_MODEL), 0.02)
    w_in = nrm((DEPTH, D_MODEL, IN_WIDTH), dsc)
    ssd_conv_w = nrm((DEPTH, SSD_CONV, SSD_CONV_DIM), SSD_CONV ** -0.5)
    ssd_conv_b = nrm((DEPTH, SSD_CONV_DIM), 0.02)
    ssd_a_log = jnp.log(jax.random.uniform(next(ks), (DEPTH, 2, SSD_HEADS), F32, 1.0, 16.0))
    dt0 = jnp.exp(jax.random.uniform(next(ks), (DEPTH, 2, SSD_HEADS), F32, math.log(1e-3), math.log(1e-1)))
    ssd_dt_bias = dt0 + jnp.log(-jnp.expm1(-dt0))
    ssd_d = 1.0 + nrm((DEPTH, SSD_HEADS), 0.1)
    ssd_norm_g = 1.0 + nrm((DEPTH, SSD_INNER), 0.02)
    q_norm_g = 1.0 + nrm((DEPTH, ATT_HEAD_DIM), 0.02)
    k_norm_g = 1.0 + nrm((DEPTH, ATT_HEAD_DIM), 0.02)
    gamma = 1.0 - 2.0 ** (-5.0 - jnp.arange(RET_HEADS, dtype=F32))
    ret_decay = (jnp.log(gamma) - jnp.log1p(-gamma))[None, None, :] + nrm((DEPTH, 2, RET_HEADS), 0.1)
    ret_norm_g = 1.0 + nrm((DEPTH, RET_INNER), 0.02)
    w_branch = nrm((DEPTH, N_BRANCHES, D_MODEL, D_MODEL), dsc)
    w_out = nrm((DEPTH, D_MODEL, D_MODEL), dsc)
    w_router = nrm((DEPTH, D_MODEL, N_EXPERTS), dsc)
    w_e1 = nrm((DEPTH, N_EXPERTS, D_MODEL, EXPERT_FF), dsc)
    w_e3 = nrm((DEPTH, N_EXPERTS, D_MODEL, EXPERT_FF), dsc)
    w_e2 = nrm((DEPTH, N_EXPERTS, EXPERT_FF, D_MODEL), EXPERT_FF ** -0.5)
    return dict(x_prompt=x_prompt, x_sample=x_sample,
                cache_attn_k=cache_attn_k, cache_attn_v=cache_attn_v,
                state_ssd_fwd=state_ssd_fwd, state_ssd_bwd=state_ssd_bwd,
                state_ret_fwd=state_ret_fwd, state_ret_bwd=state_ret_bwd,
                c=c, c_ctx=c_ctx, norm1_g=norm1_g, norm2_g=norm2_g, w_ada=w_ada, b_ada=b_ada,
                w_in=w_in, ssd_conv_w=ssd_conv_w, ssd_conv_b=ssd_conv_b, ssd_a_log=ssd_a_log,
                ssd_dt_bias=ssd_dt_bias, ssd_d=ssd_d, ssd_norm_g=ssd_norm_g,
                q_norm_g=q_norm_g, k_norm_g=k_norm_g, ret_decay=ret_decay, ret_norm_g=ret_norm_g,
                w_branch=w_branch, w_out=w_out, w_router=w_router, w_e1=w_e1, w_e3=w_e3, w_e2=w_e2)


def reference(x_prompt, x_sample, cache_attn_k, cache_attn_v, state_ssd_fwd, state_ssd_bwd,
              state_ret_fwd, state_ret_bwd, c, c_ctx, norm1_g, norm2_g, w_ada, b_ada, w_in,
              ssd_conv_w, ssd_conv_b, ssd_a_log, ssd_dt_bias, ssd_d, ssd_norm_g, q_norm_g, k_norm_g,
              ret_decay, ret_norm_g, w_branch, w_out, w_router, w_e1, w_e3, w_e2):
    def layer_params(i):
        return dict(norm1_g=norm1_g[i], norm2_g=norm2_g[i], w_ada=w_ada[i], b_ada=b_ada[i],
                    w_in=w_in[i], ssd_conv_w=ssd_conv_w[i], ssd_conv_b=ssd_conv_b[i],
                    ssd_a_log=ssd_a_log[i], ssd_dt_bias=ssd_dt_bias[i], ssd_d=ssd_d[i],
                    ssd_norm_g=ssd_norm_g[i], q_norm_g=q_norm_g[i], k_norm_g=k_norm_g[i],
                    ret_decay=ret_decay[i], ret_norm_g=ret_norm_g[i], w_branch=w_branch[i],
                    w_out=w_out[i], w_router=w_router[i], w_e1=w_e1[i], w_e3=w_e3[i], w_e2=w_e2[i])

    x = x_prompt
    ctx_states = []
    for i in range(DEPTH):
        x, st = trunk_layer(x, c_ctx[None, :], layer_params(i), None, None)
        ctx_states.append(st)
    y_prompt = x
    new_cache_attn_k = jnp.stack([st[0] for st in ctx_states], axis=1)
    new_cache_attn_v = jnp.stack([st[1] for st in ctx_states], axis=1)
    new_state_ssd_fwd = jnp.stack([st[2] for st in ctx_states], axis=1)
    new_state_ssd_bwd = jnp.stack([st[3] for st in ctx_states], axis=1)
    new_state_ret_fwd = jnp.stack([st[4] for st in ctx_states], axis=1)
    new_state_ret_bwd = jnp.stack([st[5] for st in ctx_states], axis=1)

    rope = axial_rope_tables(x_sample.shape[1], ATT_HEAD_DIM)
    x = x_sample
    for i in range(DEPTH):
        ctx = (cache_attn_k[:, i], cache_attn_v[:, i], state_ssd_fwd[:, i], state_ssd_bwd[:, i],
               state_ret_fwd[:, i], state_ret_bwd[:, i])
        x, _ = trunk_layer(x, c, layer_params(i), rope, ctx)
    y_sample = x
    return (y_prompt, y_sample, new_cache_attn_k, new_cache_attn_v, new_state_ssd_fwd,
            new_state_ssd_bwd, new_state_ret_fwd, new_state_ret_bwd)
```

```python
import functools

import jax
import jax.numpy as jnp
from jax import lax
from jax.experimental import pallas as pl
from jax.experimental.pallas import tpu as pltpu

F32 = jnp.float32
BF16 = jnp.bfloat16
EPS = 1e-6
ROPE_THETA = 10000.0
GRID_W = 64

D = 2048
CH = 128
SSD_P = 64
SSD_N = 128
SSD_G = 4
SSD_H = 32
SSD_R = SSD_H // SSD_G
ATT_D = 128
ATT_H = 16
ATT_G = 4
ATT_R = ATT_H // ATT_G
RET_H = 8
RET_K = 128
RET_V = 256
N_EXP = 16
CAP_FACTOR = 2

OFF_Z = 0
OFF_X = 2048
OFF_B = 4096
OFF_C = 4608
OFF_AQ = 5120
OFF_AK = 7168
OFF_AV = 7680
OFF_RQ = 8192
OFF_RK = 9216
OFF_RV = 10240
OFF_RG = 12288
OFF_GL = 14336
OFF_DT = 20480
NW = OFF_DT + SSD_G * 128
ORIG_DT = 5120
ORIG_AFTER_DT = 5184
ORIG_W = 20544

VMEM_LIMIT = 56 * 1024 * 1024


def _cp(sem):
    return pltpu.CompilerParams(dimension_semantics=sem, vmem_limit_bytes=VMEM_LIMIT)


def _sigmoid(x):
    return 1.0 / (1.0 + jnp.exp(-x))


def _silu(x):
    return x * _sigmoid(x)


def _softplus(x):
    return jnp.maximum(x, 0.0) + jnp.log(1.0 + jnp.exp(-jnp.abs(x)))


def _rms(x, g):
    ms = jnp.mean(x * x, axis=-1, keepdims=True)
    return x * lax.rsqrt(ms + EPS) * g


def _dot(a, b):
    return jnp.dot(a, b, preferred_element_type=F32)


def _dot_nt(a, b):
    return lax.dot_general(a, b, (((1,), (1,)), ((), ())), preferred_element_type=F32)


def _rope(x, cos, sa, sb):
    return x * cos + pltpu.roll(x, 96, 1) * sa + pltpu.roll(x, 32, 1) * sb


def _ada_kernel(c_ref, w_ref, b_ref, o_ref):
    s = _silu(c_ref[...]).astype(BF16)
    o_ref[...] = _dot(s, w_ref[...].astype(BF16)) + b_ref[...]


def _ada(cond8, w_ada, b_ada):
    depth, _, n6 = w_ada.shape
    tn = 1024
    return pl.pallas_call(
        _ada_kernel,
        grid=(depth, n6 // tn),
        in_specs=[pl.BlockSpec((8, D), lambda l, j: (0, 0)),
                  pl.BlockSpec((None, D, tn), lambda l, j: (l, 0, j)),
                  pl.BlockSpec((None, 1, tn), lambda l, j: (l, 0, j))],
        out_specs=pl.BlockSpec((None, 8, tn), lambda l, j: (l, 0, j)),
        out_shape=jax.ShapeDtypeStruct((depth, 8, n6), F32),
        compiler_params=_cp(("parallel", "parallel")),
        name="ada_mod",
    )(cond8, w_ada, b_ada.reshape(depth, 1, n6))


def _inproj_kernel(x_ref, g_ref, sh_ref, sc_ref, w_ref, o_ref, hn_ref):
    @pl.when(pl.program_id(1) == 0)
    def _():
        h = _rms(x_ref[...], g_ref[...])
        hn_ref[...] = (h * (1.0 + sc_ref[...]) + sh_ref[...]).astype(BF16)

    o_ref[...] = _dot(hn_ref[...], w_ref[...])


def _inproj(x, norm_g, mods, w_in, layer, mod_row, tm=1024, tn=512):
    t = x.shape[0]
    return pl.pallas_call(
        _inproj_kernel,
        grid=(t // tm, NW // tn),
        in_specs=[pl.BlockSpec((tm, D), lambda i, j: (i, 0)),
                  pl.BlockSpec((None, 1, D), lambda i, j: (layer, 0, 0)),
                  pl.BlockSpec((None, None, 1, D), lambda i, j: (layer, mod_row(i, tm), 0, 0)),
                  pl.BlockSpec((None, None, 1, D), lambda i, j: (layer, mod_row(i, tm), 0, 1)),
                  pl.BlockSpec((None, D, tn), lambda i, j: (layer, 0, j))],
        out_specs=pl.BlockSpec((tm, tn), lambda i, j: (i, j)),
        out_shape=jax.ShapeDtypeStruct((t, NW), F32),
        scratch_shapes=[pltpu.VMEM((tm, D), BF16)],
        compiler_params=_cp(("parallel", "arbitrary")),
        name="in_proj",
    )(x, norm_g, mods, mods, w_in)


def _ssd_kernel(*refs, seq, has_h0, out_state):
    it = iter(refs)
    x_ref, b_ref, c_ref, dt_ref = next(it), next(it), next(it), next(it)
    cwx_ref, cwb_ref, cwc_ref = next(it), next(it), next(it)
    cbx_ref, cbb_ref, cbc_ref = next(it), next(it), next(it)
    dtb_ref, alog_ref, dvec_ref = next(it), next(it), next(it)
    h0_refs = (next(it), next(it)) if has_h0 else None
    y_ref = next(it)
    hout_refs = (next(it), next(it)) if out_state else None
    xs_ref, bs_ref, cs_ref, st_ref = next(it), next(it), next(it), next(it)

    nc = seq // CH
    row1 = lax.broadcasted_iota(jnp.int32, (CH, 1), 0)

    def conv_chunk(c, carry):
        t0 = pl.multiple_of(c * CH, CH)
        pidx = pl.multiple_of(jnp.maximum(t0 - 8, 0), 8)
        nidx = pl.multiple_of(jnp.minimum(t0 + CH, seq - 8), 8)

        def conv(src, w_ref, bias_ref, dst):
            cur = src[pl.ds(t0, CH), :]
            prev_row = jnp.where(c > 0, src[pl.ds(pidx, 8), :][7:8, :], 0.0)
            next_row = jnp.where(c < nc - 1, src[pl.ds(nidx, 8), :][0:1, :], 0.0)
            xm = jnp.where(row1 == 0, prev_row, pltpu.roll(cur, 1, 0))
            xp = jnp.where(row1 == CH - 1, next_row, pltpu.roll(cur, CH - 1, 0))
            w = w_ref[...]
            v = xm * w[0:1, :] + cur * w[1:2, :] + xp * w[2:3, :] + bias_ref[...]
            dst[pl.ds(t0, CH), :] = _silu(v)

        conv(x_ref, cwx_ref, cbx_ref, xs_ref)
        conv(b_ref, cwb_ref, cbb_ref, bs_ref)
        conv(c_ref, cwc_ref, cbc_ref, cs_ref)
        return carry

    lax.fori_loop(0, nc, conv_chunk, 0)

    ri = lax.broadcasted_iota(jnp.int32, (CH, CH), 0)
    ci = lax.broadcasted_iota(jnp.int32, (CH, CH), 1)
    tri = (ri >= ci).astype(F32)
    left = ci < SSD_P
    top = row1 < SSD_P
    dtb = dtb_ref[...]
    aneg = -jnp.exp(alog_ref[...])
    dvec = dvec_ref[...]

    for d in range(2):
        if has_h0:
            st_ref[...] = h0_refs[d][...].reshape(SSD_R // 2, 2 * SSD_P, SSD_N)
        else:
            st_ref[...] = jnp.zeros_like(st_ref)
        mask = (ri >= ci) if d == 0 else (ci >= ri)

        def scan_chunk(c, carry, d=d, mask=mask):
            ch = c if d == 0 else nc - 1 - c
            t0 = pl.multiple_of(ch * CH, CH)
            dtv = _softplus(dt_ref[pl.ds(t0, CH), :] + dtb)
            a = dtv * aneg
            pref = jnp.dot(tri, a, precision=lax.Precision.HIGHEST, preferred_element_type=F32)
            tot = pref[CH - 1:CH, :]
            s = pref if d == 0 else tot - pref + a
            ea = jnp.exp(s)
            te = jnp.exp(tot - s) * dtv
            cd = jnp.exp(tot)
            s_t = s.T
            dt_t = dtv.T
            bc = bs_ref[pl.ds(t0, CH), :]
            cc = cs_ref[pl.ds(t0, CH), :]
            bcb = bc.astype(BF16)
            ccb = cc.astype(BF16)
            cb = _dot_nt(ccb, bcb)
            for pair in range(SSD_R // 2):
                la = d * SSD_R + 2 * pair
                lb = la + 1
                x2 = xs_ref[pl.ds(t0, CH), pair * 128:(pair + 1) * 128]
                dec_a = jnp.exp(jnp.where(mask, s[:, la:la + 1] - s_t[la:la + 1, :], -jnp.inf))
                dec_b = jnp.exp(jnp.where(mask, s[:, lb:lb + 1] - s_t[lb:lb + 1, :], -jnp.inf))
                w_a = (cb * dec_a * dt_t[la:la + 1, :]).astype(BF16)
                w_b = (cb * dec_b * dt_t[lb:lb + 1, :]).astype(BF16)
                x_a = jnp.where(left, x2, 0.0).astype(BF16)
                x_b = jnp.where(left, 0.0, x2).astype(BF16)
                y = _dot(w_a, x_a) + _dot(w_b, x_b)
                s2 = st_ref[pair]
                e2 = jnp.where(left, ea[:, la:la + 1], ea[:, lb:lb + 1])
                y = y + _dot_nt(ccb, s2.astype(BF16)) * e2
                te2 = jnp.where(left, te[:, la:la + 1], te[:, lb:lb + 1])
                xt = (x2 * te2).T.astype(BF16)
                cd2 = jnp.where(top, cd[:, la:la + 1], cd[:, lb:lb + 1])
                st_ref[pair] = cd2 * s2 + _dot(xt, bcb)
                if d == 0:
                    y_ref[pl.ds(t0, CH), pair * 128:(pair + 1) * 128] = (
                        y + dvec[:, pair * 128:(pair + 1) * 128] * x2)
                else:
                    y_ref[pl.ds(t0, CH), pair * 128:(pair + 1) * 128] += y
            return carry

        lax.fori_loop(0, nc, scan_chunk, 0)
        if out_state:
            hout_refs[d][...] = st_ref[...].reshape(SSD_R, SSD_P, SSD_N)


def _ssd(proj, row_off, nb, seq, conv_w, conv_b, dtb, alog, dvec, layer, h0=None, out_state=False):
    rb = row_off // seq
    has_h0 = h0 is not None
    in_specs = [
        pl.BlockSpec((seq, 512), lambda b, g: (rb + b, OFF_X // 512 + g)),
        pl.BlockSpec((seq, 128), lambda b, g: (rb + b, OFF_B // 128 + g)),
        pl.BlockSpec((seq, 128), lambda b, g: (rb + b, OFF_C // 128 + g)),
        pl.BlockSpec((seq, 128), lambda b, g: (rb + b, OFF_DT // 128 + g)),
        pl.BlockSpec((None, 3, 512), lambda b, g: (layer, 0, g)),
        pl.BlockSpec((None, 3, 128), lambda b, g: (layer, 0, D // 128 + g)),
        pl.BlockSpec((None, 3, 128), lambda b, g: (layer, 0, D // 128 + SSD_G + g)),
        pl.BlockSpec((None, 1, 512), lambda b, g: (layer, 0, g)),
        pl.BlockSpec((None, 1, 128), lambda b, g: (layer, 0, D // 128 + g)),
        pl.BlockSpec((None, 1, 128), lambda b, g: (layer, 0, D // 128 + SSD_G + g)),
        pl.BlockSpec((None, None, 1, 128), lambda b, g: (layer, g, 0, 0)),
        pl.BlockSpec((None, None, 1, 128), lambda b, g: (layer, g, 0, 0)),
        pl.BlockSpec((None, 1, 512), lambda b, g: (layer, 0, g)),
    ]
    args = [proj, proj, proj, proj, conv_w, conv_w, conv_w, conv_b, conv_b, conv_b, dtb, alog, dvec]
    if has_h0:
        st_spec = pl.BlockSpec((None, None, SSD_R, SSD_P, SSD_N), lambda b, g: (b, layer, g, 0, 0))
        in_specs += [st_spec, st_spec]
        args += [h0[0], h0[1]]
    out_specs = [pl.BlockSpec((seq, 512), lambda b, g: (b, g))]
    out_shape = [jax.ShapeDtypeStruct((nb * seq, D), F32)]
    if out_state:
        so = pl.BlockSpec((None, SSD_R, SSD_P, SSD_N), lambda b, g: (b, g, 0, 0))
        out_specs += [so, so]
        out_shape += [jax.ShapeDtypeStruct((nb, SSD_H, SSD_P, SSD_N), F32)] * 2
    return pl.pallas_call(
        functools.partial(_ssd_kernel, seq=seq, has_h0=has_h0, out_state=out_state),
        grid=(nb, SSD_G),
        in_specs=in_specs,
        out_specs=out_specs,
        out_shape=out_shape,
        scratch_shapes=[pltpu.VMEM((seq, 512), F32), pltpu.VMEM((seq, 128), F32),
                        pltpu.VMEM((seq, 128), F32), pltpu.VMEM((SSD_R // 2, 128, 128), F32)],
        compiler_params=_cp(("parallel", "parallel")),
        name="ssd_latent" if has_h0 else "ssd_ctx",
    )(*args)


def _att_kernel(*refs, seq, past, tq, latent):
    it = iter(refs)
    q_ref, k_ref, v_ref, qg_ref, kg_ref = next(it), next(it), next(it), next(it), next(it)
    if latent:
        kc_ref, vc_ref, cos_ref, sa_ref, sb_ref = next(it), next(it), next(it), next(it), next(it)
    o_ref = next(it)
    if not latent:
        ko_ref, vo_ref = next(it), next(it)
    kb_ref, vb_ref = next(it), next(it)
    qi = pl.program_id(2)

    @pl.when(qi == 0)
    def _():
        kn = _rms(k_ref[...], kg_ref[...])
        v = v_ref[...]
        if latent:
            kn = _rope(kn, cos_ref[...], sa_ref[...], sb_ref[...])
            kb_ref[0:past, :] = kc_ref[...].astype(BF16)
            vb_ref[0:past, :] = vc_ref[...].astype(BF16)
            kb_ref[past:past + seq, :] = kn.astype(BF16)
            vb_ref[past:past + seq, :] = v.astype(BF16)
        else:
            ko_ref[...] = kn
            vo_ref[...] = v
            kb_ref[...] = kn.astype(BF16)
            vb_ref[...] = v.astype(BF16)

    scale = ATT_D ** -0.5
    kb = kb_ref[...]
    vb = vb_ref[...]
    if latent:
        q0 = pl.multiple_of(qi * tq, tq)
        cos = cos_ref[pl.ds(q0, tq), :]
        sa = sa_ref[pl.ds(q0, tq), :]
        sb = sb_ref[pl.ds(q0, tq), :]
    for r in range(ATT_R):
        qn = _rms(q_ref[:, r * ATT_D:(r + 1) * ATT_D], qg_ref[...])
        if latent:
            qn = _rope(qn, cos, sa, sb)
        s = _dot_nt((qn * scale).astype(BF16), kb)
        m = jnp.max(s, axis=-1, keepdims=True)
        p = jnp.exp(s - m)
        l = jnp.sum(p, axis=-1, keepdims=True)
        o = _dot(p.astype(BF16), vb) / l
        o_ref[:, r * ATT_D:(r + 1) * ATT_D] = o.astype(BF16)


def _att(proj, row_off, nb, seq, qg, kg, layer, tq, ctx=None, rope=None):
    latent = ctx is not None
    past = ctx[0].shape[3] if latent else 0
    nq = seq // tq
    rbq = row_off // tq
    rbs = row_off // seq
    in_specs = [
        pl.BlockSpec((tq, 512), lambda b, g, q: (rbq + b * nq + q, OFF_AQ // 512 + g)),
        pl.BlockSpec((seq, 128), lambda b, g, q: (rbs + b, OFF_AK // 128 + g)),
        pl.BlockSpec((seq, 128), lambda b, g, q: (rbs + b, OFF_AV // 128 + g)),
        pl.BlockSpec((None, 1, ATT_D), lambda b, g, q: (layer, 0, 0)),
        pl.BlockSpec((None, 1, ATT_D), lambda b, g, q: (layer, 0, 0)),
    ]
    args = [proj, proj, proj, qg, kg]
    if latent:
        cs = pl.BlockSpec((None, None, None, past, ATT_D), lambda b, g, q: (b, layer, g, 0, 0))
        ts = pl.BlockSpec((seq, ATT_D), lambda b, g, q: (0, 0))
        in_specs += [cs, cs, ts, ts, ts]
        args += [ctx[0], ctx[1], rope[0], rope[1], rope[2]]
    out_specs = [pl.BlockSpec((tq, 512), lambda b, g, q: (b * nq + q, g))]
    out_shape = [jax.ShapeDtypeStruct((nb * seq, D), BF16)]
    if not latent:
        so = pl.BlockSpec((None, None, seq, ATT_D), lambda b, g, q: (b, g, 0, 0))
        out_specs += [so, so]
        out_shape += [jax.ShapeDtypeStruct((nb, ATT_G, seq, ATT_D), F32)] * 2
    return pl.pallas_call(
        functools.partial(_att_kernel, seq=seq, past=past, tq=tq, latent=latent),
        grid=(nb, ATT_G, nq),
        in_specs=in_specs,
        out_specs=out_specs,
        out_shape=out_shape,
        scratch_shapes=[pltpu.VMEM((past + seq, ATT_D), BF16), pltpu.VMEM((past + seq, ATT_D), BF16)],
        compiler_params=_cp(("parallel", "parallel", "arbitrary")),
        name="att_latent" if latent else "att_ctx",
    )(*args)


def _ret_kernel(*refs, seq, latent, out_state):
    it = iter(refs)
    q_ref, k_ref, v_ref, rg_ref, rd_ref, ng_ref = (next(it) for _ in range(6))
    if latent:
        cos_ref, sa_ref, sb_ref, s0f_ref, s0b_ref = (next(it) for _ in range(5))
    o_ref = next(it)
    if out_state:
        sout_refs = (next(it), next(it))
    of_ref, st_ref = next(it), next(it)

    nc = seq // CH
    ri = lax.broadcasted_iota(jnp.int32, (CH, CH), 0)
    ci = lax.broadcasted_iota(jnp.int32, (CH, CH), 1)
    pos = ri.astype(F32)
    diff = (ri - ci).astype(F32)
    scale = RET_K ** -0.5
    rd = rd_ref[...]
    lg_all = jnp.minimum(rd, 0.0) - jnp.log(1.0 + jnp.exp(-jnp.abs(rd)))

    for d in range(2):
        lg = lg_all[d:d + 1, :]
        if d == 0:
            dm = jnp.exp(jnp.where(ri >= ci, diff * lg, -jnp.inf))
            kte = jnp.exp((CH - 1.0 - pos) * lg)
            qfs = jnp.exp((pos + 1.0) * lg)
        else:
            dm = jnp.exp(jnp.where(ci > ri, -diff * lg, -jnp.inf))
            kte = jnp.exp(pos * lg)
            qfs = jnp.exp((CH - pos) * lg)
        cdec = jnp.exp(CH * lg)
        cdec2 = jnp.concatenate([cdec, cdec], axis=1)
        if latent:
            st_ref[...] = (s0f_ref if d == 0 else s0b_ref)[...]
        else:
            st_ref[...] = jnp.zeros_like(st_ref)

        def chunk(c, carry, d=d, dm=dm, kte=kte, qfs=qfs, cdec2=cdec2):
            ch = c if d == 0 else nc - 1 - c
            t0 = pl.multiple_of(ch * CH, CH)
            q = q_ref[pl.ds(t0, CH), :] * scale
            k = k_ref[pl.ds(t0, CH), :]
            if latent:
                cos = cos_ref[pl.ds(t0, CH), :]
                sa = sa_ref[pl.ds(t0, CH), :]
                sb = sb_ref[pl.ds(t0, CH), :]
                q = _rope(q, cos, sa, sb)
                k = _rope(k, cos, sa, sb)
            vb = v_ref[pl.ds(t0, CH), :].astype(BF16)
            sc = _dot_nt(q.astype(BF16), k.astype(BF16)) * dm
            st = st_ref[...]
            o = _dot(sc.astype(BF16), vb) + _dot((q * qfs).astype(BF16), st.astype(BF16))
            st_ref[...] = cdec2 * st + _dot((k * kte).T.astype(BF16), vb)
            if d == 0:
                of_ref[pl.ds(t0, CH), :] = o
            else:
                tot = of_ref[pl.ds(t0, CH), :] + o
                y = _silu(rg_ref[pl.ds(t0, CH), :]) * _rms(tot, ng_ref[...])
                o_ref[pl.ds(t0, CH), :] = y.astype(BF16)
            return carry

        lax.fori_loop(0, nc, chunk, 0)
        if out_state:
            sout_refs[d][...] = st_ref[...]


def _ret(proj, row_off, nb, seq, rd, ng, layer, rope=None, s0=None, out_state=False):
    latent = s0 is not None
    rb = row_off // seq
    in_specs = [
        pl.BlockSpec((seq, RET_K), lambda b, h: (rb + b, OFF_RQ // RET_K + h)),
        pl.BlockSpec((seq, RET_K), lambda b, h: (rb + b, OFF_RK // RET_K + h)),
        pl.BlockSpec((seq, RET_V), lambda b, h: (rb + b, OFF_RV // RET_V + h)),
        pl.BlockSpec((seq, RET_V), lambda b, h: (rb + b, OFF_RG // RET_V + h)),
        pl.BlockSpec((None, None, 2, 128), lambda b, h: (layer, h, 0, 0)),
        pl.BlockSpec((None, 1, RET_V), lambda b, h: (layer, 0, h)),
    ]
    args = [proj, proj, proj, proj, rd, ng]
    if latent:
        ts = pl.BlockSpec((seq, RET_K), lambda b, h: (0, 0))
        ss = pl.BlockSpec((None, None, None, RET_K, RET_V), lambda b, h: (b, layer, h, 0, 0))
        in_specs += [ts, ts, ts, ss, ss]
        args += [rope[0], rope[1], rope[2], s0[0], s0[1]]
    out_specs = [pl.BlockSpec((seq, RET_V), lambda b, h: (b, h))]
    out_shape = [jax.ShapeDtypeStruct((nb * seq, D), BF16)]
    if out_state:
        so = pl.BlockSpec((None, None, RET_K, RET_V), lambda b, h: (b, h, 0, 0))
        out_specs += [so, so]
        out_shape += [jax.ShapeDtypeStruct((nb, RET_H, RET_K, RET_V), F32)] * 2
    return pl.pallas_call(
        functools.partial(_ret_kernel, seq=seq, latent=latent, out_state=out_state),
        grid=(nb, RET_H),
        in_specs=in_specs,
        out_specs=out_specs,
        out_shape=out_shape,
        scratch_shapes=[pltpu.VMEM((seq, RET_V), F32), pltpu.VMEM((RET_K, RET_V), F32)],
        compiler_params=_cp(("parallel", "parallel")),
        name="ret_latent" if latent else "ret_ctx",
    )(*args)


def _merge_kernel(ys_ref, z_ref, ng_ref, ya_ref, yr_ref, g0_ref, g1_ref, g2_ref, wb_ref, o_ref, y0_ref):
    @pl.when(pl.program_id(1) == 0)
    def _():
        y = ys_ref[...] * _silu(z_ref[...])
        y0_ref[...] = _rms(y, ng_ref[...]).astype(BF16)

    acc = _sigmoid(g0_ref[...]) * _dot(y0_ref[...], wb_ref[0])
    acc += _sigmoid(g1_ref[...]) * _dot(ya_ref[...], wb_ref[1])
    acc += _sigmoid(g2_ref[...]) * _dot(yr_ref[...], wb_ref[2])
    o_ref[...] = acc.astype(BF16)


def _merge(y_ssd, proj, ssd_ng, y_att, y_ret, w_branch, layer, tm=512, tn=512):
    t = y_ssd.shape[0]
    gl = OFF_GL // tn
    nj = D // tn
    return pl.pallas_call(
        _merge_kernel,
        grid=(t // tm, nj),
        in_specs=[pl.BlockSpec((tm, D), lambda i, j: (i, 0)),
                  pl.BlockSpec((tm, D), lambda i, j: (i, OFF_Z // D)),
                  pl.BlockSpec((None, 1, D), lambda i, j: (layer, 0, 0)),
                  pl.BlockSpec((tm, D), lambda i, j: (i, 0)),
                  pl.BlockSpec((tm, D), lambda i, j: (i, 0)),
                  pl.BlockSpec((tm, tn), lambda i, j: (i, gl + j)),
                  pl.BlockSpec((tm, tn), lambda i, j: (i, gl + nj + j)),
                  pl.BlockSpec((tm, tn), lambda i, j: (i, gl + 2 * nj + j)),
                  pl.BlockSpec((None, 3, D, tn), lambda i, j: (layer, 0, 0, j))],
        out_specs=pl.BlockSpec((tm, tn), lambda i, j: (i, j)),
        out_shape=jax.ShapeDtypeStruct((t, D), BF16),
        scratch_shapes=[pltpu.VMEM((tm, D), BF16)],
        compiler_params=_cp(("parallel", "arbitrary")),
        name="merge",
    )(y_ssd, proj, ssd_ng, y_att, y_ret, proj, proj, proj, w_branch)


def _outproj_kernel(m_ref, w_ref, x_ref, g1_ref, ng_ref, sh_ref, sc_ref, wr_ref, x1_ref, h2_ref, aff_ref):
    x1 = x_ref[...] + g1_ref[...] * _dot(m_ref[...], w_ref[...])
    x1_ref[...] = x1
    h = _rms(x1, ng_ref[...]) * (1.0 + sc_ref[...]) + sh_ref[...]
    h2_ref[...] = h.astype(BF16)
    logits = jnp.dot(h, wr_ref[...], precision=lax.Precision.HIGHEST, preferred_element_type=F32)
    lane = lax.broadcasted_iota(jnp.int32, logits.shape, 1)
    logits = jnp.where(lane < N_EXP, logits, -jnp.inf)
    e = jnp.exp(logits - jnp.max(logits, axis=-1, keepdims=True))
    aff_ref[...] = e / jnp.sum(e, axis=-1, keepdims=True)


def _outproj(merged, w_out, x, mods, norm_g, w_router, layer, mod_row, tm=512):
    t = x.shape[0]

    def ms(col):
        return pl.BlockSpec((None, None, 1, D), lambda i: (layer, mod_row(i, tm), 0, col))

    return pl.pallas_call(
        _outproj_kernel,
        grid=(t // tm,),
        in_specs=[pl.BlockSpec((tm, D), lambda i: (i, 0)),
                  pl.BlockSpec((None, D, D), lambda i: (layer, 0, 0)),
                  pl.BlockSpec((tm, D), lambda i: (i, 0)),
                  ms(2),
                  pl.BlockSpec((None, 1, D), lambda i: (layer, 0, 0)),
                  ms(3), ms(4),
                  pl.BlockSpec((None, D, 128), lambda i: (layer, 0, 0))],
        out_specs=[pl.BlockSpec((tm, D), lambda i: (i, 0)),
                   pl.BlockSpec((tm, D), lambda i: (i, 0)),
                   pl.BlockSpec((tm, 128), lambda i: (i, 0))],
        out_shape=[jax.ShapeDtypeStruct((t, D), F32),
                   jax.ShapeDtypeStruct((t, D), BF16),
                   jax.ShapeDtypeStruct((t, 128), F32)],
        compiler_params=_cp(("parallel",)),
        name="out_proj",
    )(merged, w_out, x, mods, norm_g, mods, mods, w_router)


def _ffn_kernel(xe_ref, w1_ref, w3_ref, w2_ref, gate_ref, o_ref, acc_ref):
    f = pl.program_id(2)
    x = xe_ref[...]
    hid = (_silu(_dot(x, w1_ref[...])) * _dot(x, w3_ref[...])).astype(BF16)
    contrib = _dot(hid, w2_ref[...])

    @pl.when(f == 0)
    def _():
        acc_ref[...] = contrib

    @pl.when(f > 0)
    def _():
        acc_ref[...] += contrib

    @pl.when(f == pl.num_programs(2) - 1)
    def _():
        o_ref[...] = acc_ref[...] * gate_ref[...]


def _ffn(xe, gate, w1, w3, w2, layer, tc=512, tf=512):
    n_e, cap, _ = xe.shape
    ff = w1.shape[-1]
    return pl.pallas_call(
        _ffn_kernel,
        grid=(n_e, cap // tc, ff // tf),
        in_specs=[pl.BlockSpec((None, tc, D), lambda e, c, f: (e, c, 0)),
                  pl.BlockSpec((None, None, D, tf), lambda e, c, f: (layer, e, 0, f)),
                  pl.BlockSpec((None, None, D, tf), lambda e, c, f: (layer, e, 0, f)),
                  pl.BlockSpec((None, None, tf, D), lambda e, c, f: (layer, e, f, 0)),
                  pl.BlockSpec((None, tc, 1), lambda e, c, f: (e, c, 0))],
        out_specs=pl.BlockSpec((None, tc, D), lambda e, c, f: (e, c, 0)),
        out_shape=jax.ShapeDtypeStruct((n_e, cap, D), F32),
        scratch_shapes=[pltpu.VMEM((tc, D), F32)],
        compiler_params=_cp(("parallel", "parallel", "arbitrary")),
        name="expert_ffn",
    )(xe, w1, w3, w2, gate.reshape(n_e, cap, 1))


def _rope_tables(n_tokens):
    rows_n = n_tokens // GRID_W
    row = jnp.repeat(jnp.arange(rows_n, dtype=F32), GRID_W)
    col = jnp.tile(jnp.arange(GRID_W, dtype=F32), rows_n)
    n_freq = ATT_D // 4
    inv = ROPE_THETA ** (-jnp.arange(n_freq, dtype=F32) / n_freq)
    ang_r = row[:, None] * inv
    ang_c = col[:, None] * inv
    zeros = jnp.zeros_like(ang_r)
    cos = jnp.concatenate([jnp.cos(ang_r)] * 2 + [jnp.cos(ang_c)] * 2, axis=1)
    sa = jnp.concatenate([-jnp.sin(ang_r), zeros, -jnp.sin(ang_c), zeros], axis=1)
    sb = jnp.concatenate([zeros, jnp.sin(ang_r), zeros, jnp.sin(ang_c)], axis=1)
    return cos, sa, sb


def _group_dt_lanes(v):
    depth = v.shape[0]
    v = v.reshape(depth, 2, SSD_G, SSD_R).transpose(0, 2, 1, 3).reshape(depth, SSD_G, 1, 2 * SSD_R)
    return jnp.pad(v, ((0, 0), (0, 0), (0, 0), (0, 128 - 2 * SSD_R)))


def _prep_w_in(w_in):
    depth = w_in.shape[0]
    dt = w_in[:, :, ORIG_DT:ORIG_AFTER_DT].reshape(depth, D, 2, SSD_G, SSD_R)
    dt = dt.transpose(0, 1, 3, 2, 4).reshape(depth, D, SSD_G, 2 * SSD_R)
    dt = jnp.pad(dt, ((0, 0), (0, 0), (0, 0), (0, 128 - 2 * SSD_R))).reshape(depth, D, SSD_G * 128)
    w = jnp.concatenate([w_in[:, :, :ORIG_DT], w_in[:, :, ORIG_AFTER_DT:], dt], axis=2)
    return w.astype(BF16)


def kernel(x_prompt, x_sample, cache_attn_k, cache_attn_v, state_ssd_fwd, state_ssd_bwd, state_ret_fwd, state_ret_bwd, c, c_ctx, norm1_g, norm2_g, w_ada, b_ada, w_in, ssd_conv_w, ssd_conv_b, ssd_a_log, ssd_dt_bias, ssd_d, ssd_norm_g, q_norm_g, k_norm_g, ret_decay, ret_norm_g, w_branch, w_out, w_router, w_e1, w_e3, w_e2):
    depth = w_in.shape[0]
    pb, pseq, _ = x_prompt.shape
    sb, sseq, _ = x_sample.shape
    n_p = pb * pseq
    n_s = sb * sseq
    assert n_p % 1024 == 0 and sseq % 1024 == 0 and 1 + sb <= 8

    def mod_row(i, tm):
        return jnp.where(i < n_p // tm, 0, 1 + (i * tm - n_p) // sseq)

    w_in_b = _prep_w_in(w_in)
    w_branch_b = w_branch.astype(BF16)
    w_out_b = w_out.astype(BF16)
    w_e1_b, w_e3_b, w_e2_b = w_e1.astype(BF16), w_e3.astype(BF16), w_e2.astype(BF16)
    w_router_p = jnp.pad(w_router, ((0, 0), (0, 0), (0, 128 - N_EXP)))
    dtb = _group_dt_lanes(ssd_dt_bias)
    alog = _group_dt_lanes(ssd_a_log)
    dvec = jnp.repeat(ssd_d, SSD_P, axis=1).reshape(depth, 1, D)
    rd = jnp.broadcast_to(ret_decay.transpose(0, 2, 1)[..., None], (depth, RET_H, 2, 128))
    rope = _rope_tables(sseq)
    n1g = norm1_g.reshape(depth, 1, D)
    n2g = norm2_g.reshape(depth, 1, D)
    ssd_ng = ssd_norm_g.reshape(depth, 1, D)
    ret_ng = ret_norm_g.reshape(depth, 1, D)
    qg = q_norm_g.reshape(depth, 1, ATT_D)
    kg = k_norm_g.reshape(depth, 1, ATT_D)

    cond8 = jnp.zeros((8, D), F32).at[0].set(c_ctx).at[1:1 + sb].set(c)
    mods = _ada(cond8, w_ada, b_ada).reshape(depth, 8, 1, 6 * D)

    x = jnp.concatenate([x_prompt.reshape(n_p, D), x_sample.reshape(n_s, D)], axis=0)
    new_k, new_v, new_hf, new_hb, new_sf, new_sb = [], [], [], [], [], []
    for l in range(depth):
        proj = _inproj(x, n1g, mods, w_in_b, l, mod_row)

        ys_p, hf, hb = _ssd(proj, 0, pb, pseq, ssd_conv_w, ssd_conv_b.reshape(depth, 1, -1), dtb, alog, dvec,
                            l, out_state=True)
        (ys_s,) = _ssd(proj, n_p, sb, sseq, ssd_conv_w, ssd_conv_b.reshape(depth, 1, -1), dtb, alog, dvec,
                       l, h0=(state_ssd_fwd, state_ssd_bwd))
        ya_p, kn, vn = _att(proj, 0, pb, pseq, qg, kg, l, tq=pseq)
        (ya_s,) = _att(proj, n_p, sb, sseq, qg, kg, l, tq=256, ctx=(cache_attn_k, cache_attn_v), rope=rope)
        yr_p, sf, sbk = _ret(proj, 0, pb, pseq, rd, ret_ng, l, out_state=True)
        (yr_s,) = _ret(proj, n_p, sb, sseq, rd, ret_ng, l, rope=rope, s0=(state_ret_fwd, state_ret_bwd))
        new_k.append(kn), new_v.append(vn), new_hf.append(hf), new_hb.append(hb)
        new_sf.append(sf), new_sb.append(sbk)

        y_ssd = jnp.concatenate([ys_p, ys_s], axis=0)
        y_att = jnp.concatenate([ya_p, ya_s], axis=0)
        y_ret = jnp.concatenate([yr_p, yr_s], axis=0)
        merged = _merge(y_ssd, proj, ssd_ng, y_att, y_ret, w_branch_b, l)
        x1, h2, aff = _outproj(merged, w_out_b, x, mods, n2g, w_router_p, l, mod_row)

        outs = []
        for lo, n in ((0, n_p), (n_p, n_s)):
            cap = CAP_FACTOR * n // N_EXP
            gate, idx = lax.top_k(aff[lo:lo + n, :N_EXP].T, cap)
            xe = h2[lo:lo + n][idx]
            ye = _ffn(xe, gate, w_e1_b, w_e3_b, w_e2_b, l, tc=min(cap, 1024))
            outs.append(jnp.zeros((n, D), F32).at[idx.reshape(-1)].add(ye.reshape(-1, D)))
        ffn_out = jnp.concatenate(outs, axis=0)
        g2r = mods[l, :, 0, 5 * D:]
        g2 = jnp.concatenate([jnp.broadcast_to(g2r[0:1], (n_p, D)), jnp.repeat(g2r[1:1 + sb], sseq, axis=0)])
        x = x1 + g2 * ffn_out

    y_prompt = x[:n_p].reshape(pb, pseq, D)
    y_sample = x[n_p:].reshape(sb, sseq, D)
    return (y_prompt, y_sample, jnp.stack(new_k, axis=1), jnp.stack(new_v, axis=1),
            jnp.stack(new_hf, axis=1), jnp.stack(new_hb, axis=1),
            jnp.stack(new_sf, axis=1), jnp.stack(new_sb, axis=1))
```

```python
import functools

import jax
import jax.numpy as jnp
from jax import lax
from jax.experimental import pallas as pl
from jax.experimental.pallas import tpu as pltpu

F32 = jnp.float32
BF16 = jnp.bfloat16
EPS = 1e-6
ROPE_THETA = 10000.0
GRID_W = 64

D = 2048
CH = 128
SSD_P = 64
SSD_N = 128
SSD_G = 4
SSD_H = 32
SSD_R = SSD_H // SSD_G
ATT_D = 128
ATT_H = 16
ATT_G = 4
ATT_R = ATT_H // ATT_G
RET_H = 8
RET_K = 128
RET_V = 256
N_EXP = 16
CAP_FACTOR = 2

OFF_Z = 0
OFF_X = 2048
OFF_B = 4096
OFF_C = 4608
OFF_AQ = 5120
OFF_AK = 7168
OFF_AV = 7680
OFF_RQ = 8192
OFF_RK = 9216
OFF_RV = 10240
OFF_RG = 12288
OFF_GL = 14336
OFF_DT = 20480
NW = OFF_DT + SSD_G * 128
ORIG_DT = 5120
ORIG_AFTER_DT = 5184
ORIG_W = 20544

VMEM_LIMIT = 56 * 1024 * 1024


def _cp(sem):
    return pltpu.CompilerParams(dimension_semantics=sem, vmem_limit_bytes=VMEM_LIMIT)


def _sigmoid(x):
    return 1.0 / (1.0 + jnp.exp(-x))


def _silu(x):
    return x * _sigmoid(x)


def _softplus(x):
    return jnp.maximum(x, 0.0) + jnp.log(1.0 + jnp.exp(-jnp.abs(x)))


def _rms(x, g):
    ms = jnp.mean(x * x, axis=-1, keepdims=True)
    return x * lax.rsqrt(ms + EPS) * g


def _dot(a, b):
    return jnp.dot(a, b, preferred_element_type=F32)


def _dot_nt(a, b):
    return lax.dot_general(a, b, (((1,), (1,)), ((), ())), preferred_element_type=F32)


def _rope(x, cos, sa, sb):
    return x * cos + pltpu.roll(x, 96, 1) * sa + pltpu.roll(x, 32, 1) * sb


def _ada_kernel(c_ref, w_ref, b_ref, o_ref):
    s = _silu(c_ref[...]).astype(BF16)
    o_ref[...] = _dot(s, w_ref[...].astype(BF16)) + b_ref[...]


def _ada(cond8, w_ada, b_ada):
    depth, _, n6 = w_ada.shape
    tn = 1024
    return pl.pallas_call(
        _ada_kernel,
        grid=(depth, n6 // tn),
        in_specs=[pl.BlockSpec((8, D), lambda l, j: (0, 0)),
                  pl.BlockSpec((None, D, tn), lambda l, j: (l, 0, j)),
                  pl.BlockSpec((None, 1, tn), lambda l, j: (l, 0, j))],
        out_specs=pl.BlockSpec((None, 8, tn), lambda l, j: (l, 0, j)),
        out_shape=jax.ShapeDtypeStruct((depth, 8, n6), F32),
        compiler_params=_cp(("parallel", "parallel")),
        name="ada_mod",
    )(cond8, w_ada, b_ada.reshape(depth, 1, n6))


def _inproj_kernel(x_ref, g_ref, sh_ref, sc_ref, w_ref, o_ref, hn_ref):
    @pl.when(pl.program_id(1) == 0)
    def _():
        h = _rms(x_ref[...], g_ref[...])
        hn_ref[...] = (h * (1.0 + sc_ref[...]) + sh_ref[...]).astype(BF16)

    o_ref[...] = _dot(hn_ref[...], w_ref[...])


def _inproj(x, norm_g, mods, w_in, layer, mod_row, tm=1024, tn=512):
    t = x.shape[0]
    return pl.pallas_call(
        _inproj_kernel,
        grid=(t // tm, NW // tn),
        in_specs=[pl.BlockSpec((tm, D), lambda i, j: (i, 0)),
                  pl.BlockSpec((None, 1, D), lambda i, j: (layer, 0, 0)),
                  pl.BlockSpec((None, None, 1, D), lambda i, j: (layer, mod_row(i, tm), 0, 0)),
                  pl.BlockSpec((None, None, 1, D), lambda i, j: (layer, mod_row(i, tm), 0, 1)),
                  pl.BlockSpec((None, D, tn), lambda i, j: (layer, 0, j))],
        out_specs=pl.BlockSpec((tm, tn), lambda i, j: (i, j)),
        out_shape=jax.ShapeDtypeStruct((t, NW), F32),
        scratch_shapes=[pltpu.VMEM((tm, D), BF16)],
        compiler_params=_cp(("parallel", "arbitrary")),
        name="in_proj",
    )(x, norm_g, mods, mods, w_in)


def _ssd_kernel(*refs, seq, has_h0, out_state):
    it = iter(refs)
    x_ref, b_ref, c_ref, dt_ref = next(it), next(it), next(it), next(it)
    cwx_ref, cwb_ref, cwc_ref = next(it), next(it), next(it)
    cbx_ref, cbb_ref, cbc_ref = next(it), next(it), next(it)
    dtb_ref, alog_ref, dvec_ref = next(it), next(it), next(it)
    h0_refs = (next(it), next(it)) if has_h0 else None
    y_ref = next(it)
    hout_refs = (next(it), next(it)) if out_state else None
    xs_ref, bs_ref, cs_ref, st_ref = next(it), next(it), next(it), next(it)

    nc = seq // CH
    row1 = lax.broadcasted_iota(jnp.int32, (CH, 1), 0)

    def conv_chunk(c, carry):
        t0 = pl.multiple_of(c * CH, CH)
        pidx = pl.multiple_of(jnp.maximum(t0 - 8, 0), 8)
        nidx = pl.multiple_of(jnp.minimum(t0 + CH, seq - 8), 8)

        def conv(src, w_ref, bias_ref, dst):
            cur = src[pl.ds(t0, CH), :]
            prev_row = jnp.where(c > 0, src[pl.ds(pidx, 8), :][7:8, :], 0.0)
            next_row = jnp.where(c < nc - 1, src[pl.ds(nidx, 8), :][0:1, :], 0.0)
            xm = jnp.where(row1 == 0, prev_row, pltpu.roll(cur, 1, 0))
            xp = jnp.where(row1 == CH - 1, next_row, pltpu.roll(cur, CH - 1, 0))
            w = w_ref[...]
            v = xm * w[0:1, :] + cur * w[1:2, :] + xp * w[2:3, :] + bias_ref[...]
            dst[pl.ds(t0, CH), :] = _silu(v)

        conv(x_ref, cwx_ref, cbx_ref, xs_ref)
        conv(b_ref, cwb_ref, cbb_ref, bs_ref)
        conv(c_ref, cwc_ref, cbc_ref, cs_ref)
        return carry

    lax.fori_loop(0, nc, conv_chunk, 0)

    ri = lax.broadcasted_iota(jnp.int32, (CH, CH), 0)
    ci = lax.broadcasted_iota(jnp.int32, (CH, CH), 1)
    tri = (ri >= ci).astype(F32)
    left = ci < SSD_P
    top = row1 < SSD_P
    dtb = dtb_ref[...]
    aneg = -jnp.exp(alog_ref[...])
    dvec = dvec_ref[...]

    for d in range(2):
        if has_h0:
            st_ref[...] = h0_refs[d][...].reshape(SSD_R // 2, 2 * SSD_P, SSD_N)
        else:
            st_ref[...] = jnp.zeros_like(st_ref)
        mask = (ri >= ci) if d == 0 else (ci >= ri)

        def scan_chunk(c, carry, d=d, mask=mask):
            ch = c if d == 0 else nc - 1 - c
            t0 = pl.multiple_of(ch * CH, CH)
            dtv = _softplus(dt_ref[pl.ds(t0, CH), :] + dtb)
            a = dtv * aneg
            pref = jnp.dot(tri, a, precision=lax.Precision.HIGHEST, preferred_element_type=F32)
            tot = pref[CH - 1:CH, :]
            s = pref if d == 0 else tot - pref + a
            ea = jnp.exp(s)
            te = jnp.exp(tot - s) * dtv
            cd = jnp.exp(tot)
            s_t = s.T
            dt_t = dtv.T
            bc = bs_ref[pl.ds(t0, CH), :]
            cc = cs_ref[pl.ds(t0, CH), :]
            bcb = bc.astype(BF16)
            ccb = cc.astype(BF16)
            cb = _dot_nt(ccb, bcb)
            for pair in range(SSD_R // 2):
                la = d * SSD_R + 2 * pair
                lb = la + 1
                x2 = xs_ref[pl.ds(t0, CH), pair * 128:(pair + 1) * 128]
                dec_a = jnp.exp(jnp.where(mask, s[:, la:la + 1] - s_t[la:la + 1, :], -jnp.inf))
                dec_b = jnp.exp(jnp.where(mask, s[:, lb:lb + 1] - s_t[lb:lb + 1, :], -jnp.inf))
                w_a = (cb * dec_a * dt_t[la:la + 1, :]).astype(BF16)
                w_b = (cb * dec_b * dt_t[lb:lb + 1, :]).astype(BF16)
                x_a = jnp.where(left, x2, 0.0).astype(BF16)
                x_b = jnp.where(left, 0.0, x2).astype(BF16)
                y = _dot(w_a, x_a) + _dot(w_b, x_b)
                s2 = st_ref[pair]
                e2 = jnp.where(left, ea[:, la:la + 1], ea[:, lb:lb + 1])
                y = y + _dot_nt(ccb, s2.astype(BF16)) * e2
                te2 = jnp.where(left, te[:, la:la + 1], te[:, lb:lb + 1])
                xt = (x2 * te2).T.astype(BF16)
                cd2 = jnp.where(top, cd[:, la:la + 1], cd[:, lb:lb + 1])
                st_ref[pair] = cd2 * s2 + _dot(xt, bcb)
                if d == 0:
                    y_ref[pl.ds(t0, CH), pair * 128:(pair + 1) * 128] = (
                        y + dvec[:, pair * 128:(pair + 1) * 128] * x2)
                else:
                    y_ref[pl.ds(t0, CH), pair * 128:(pair + 1) * 128] += y
            return carry

        lax.fori_loop(0, nc, scan_chunk, 0)
        if out_state:
            hout_refs[d][...] = st_ref[...].reshape(SSD_R, SSD_P, SSD_N)


def _ssd(proj, row_off, nb, seq, conv_w, conv_b, dtb, alog, dvec, layer, h0=None, out_state=False):
    rb = row_off // seq
    has_h0 = h0 is not None
    in_specs = [
        pl.BlockSpec((seq, 512), lambda b, g: (rb + b, OFF_X // 512 + g)),
        pl.BlockSpec((seq, 128), lambda b, g: (rb + b, OFF_B // 128 + g)),
        pl.BlockSpec((seq, 128), lambda b, g: (rb + b, OFF_C // 128 + g)),
        pl.BlockSpec((seq, 128), lambda b, g: (rb + b, OFF_DT // 128 + g)),
        pl.BlockSpec((None, 3, 512), lambda b, g: (layer, 0, g)),
        pl.BlockSpec((None, 3, 128), lambda b, g: (layer, 0, D // 128 + g)),
        pl.BlockSpec((None, 3, 128), lambda b, g: (layer, 0, D // 128 + SSD_G + g)),
        pl.BlockSpec((None, 1, 512), lambda b, g: (layer, 0, g)),
        pl.BlockSpec((None, 1, 128), lambda b, g: (layer, 0, D // 128 + g)),
        pl.BlockSpec((None, 1, 128), lambda b, g: (layer, 0, D // 128 + SSD_G + g)),
        pl.BlockSpec((None, None, 1, 128), lambda b, g: (layer, g, 0, 0)),
        pl.BlockSpec((None, None, 1, 128), lambda b, g: (layer, g, 0, 0)),
        pl.BlockSpec((None, 1, 512), lambda b, g: (layer, 0, g)),
    ]
    args = [proj, proj, proj, proj, conv_w, conv_w, conv_w, conv_b, conv_b, conv_b, dtb, alog, dvec]
    if has_h0:
        st_spec = pl.BlockSpec((None, None, SSD_R, SSD_P, SSD_N), lambda b, g: (b, layer, g, 0, 0))
        in_specs += [st_spec, st_spec]
        args += [h0[0], h0[1]]
    out_specs = [pl.BlockSpec((seq, 512), lambda b, g: (b, g))]
    out_shape = [jax.ShapeDtypeStruct((nb * seq, D), F32)]
    if out_state:
        so = pl.BlockSpec((None, SSD_R, SSD_P, SSD_N), lambda b, g: (b, g, 0, 0))
        out_specs += [so, so]
        out_shape += [jax.ShapeDtypeStruct((nb, SSD_H, SSD_P, SSD_N), F32)] * 2
    return pl.pallas_call(
        functools.partial(_ssd_kernel, seq=seq, has_h0=has_h0, out_state=out_state),
        grid=(nb, SSD_G),
        in_specs=in_specs,
        out_specs=out_specs,
        out_shape=out_shape,
        scratch_shapes=[pltpu.VMEM((seq, 512), F32), pltpu.VMEM((seq, 128), F32),
                        pltpu.VMEM((seq, 128), F32), pltpu.VMEM((SSD_R // 2, 128, 128), F32)],
        compiler_params=_cp(("parallel", "parallel")),
        name="ssd_latent" if has_h0 else "ssd_ctx",
    )(*args)


def _att_kernel(*refs, seq, past, tq, latent):
    it = iter(refs)
    q_ref, k_ref, v_ref, qg_ref, kg_ref = next(it), next(it), next(it), next(it), next(it)
    if latent:
        kc_ref, vc_ref, cos_ref, sa_ref, sb_ref = next(it), next(it), next(it), next(it), next(it)
    o_ref = next(it)
    if not latent:
        ko_ref, vo_ref = next(it), next(it)
    kb_ref, vb_ref = next(it), next(it)
    qi = pl.program_id(2)

    @pl.when(qi == 0)
    def _():
        kn = _rms(k_ref[...], kg_ref[...])
        v = v_ref[...]
        if latent:
            kn = _rope(kn, cos_ref[...], sa_ref[...], sb_ref[...])
            kb_ref[0:past, :] = kc_ref[...].astype(BF16)
            vb_ref[0:past, :] = vc_ref[...].astype(BF16)
            kb_ref[past:past + seq, :] = kn.astype(BF16)
            vb_ref[past:past + seq, :] = v.astype(BF16)
        else:
            ko_ref[...] = kn
            vo_ref[...] = v
            kb_ref[...] = kn.astype(BF16)
            vb_ref[...] = v.astype(BF16)

    scale = ATT_D ** -0.5 * 1.4426950408889634
    kb = kb_ref[...]
    vb = vb_ref[...]
    if latent:
        q0 = pl.multiple_of(qi * tq, tq)
        cos = cos_ref[pl.ds(q0, tq), :]
        sa = sa_ref[pl.ds(q0, tq), :]
        sb = sb_ref[pl.ds(q0, tq), :]
    for r in range(ATT_R):
        qn = _rms(q_ref[:, r * ATT_D:(r + 1) * ATT_D], qg_ref[...])
        if latent:
            qn = _rope(qn, cos, sa, sb)
        s = _dot_nt((qn * scale).astype(BF16), kb)
        m = jnp.max(s, axis=-1, keepdims=True)
        p = jnp.exp2(s - m)
        l = jnp.sum(p, axis=-1, keepdims=True)
        o = _dot(p.astype(BF16), vb) / l
        o_ref[:, r * ATT_D:(r + 1) * ATT_D] = o.astype(BF16)


def _att(proj, row_off, nb, seq, qg, kg, layer, tq, ctx=None, rope=None):
    latent = ctx is not None
    past = ctx[0].shape[3] if latent else 0
    nq = seq // tq
    rbq = row_off // tq
    rbs = row_off // seq
    in_specs = [
        pl.BlockSpec((tq, 512), lambda b, g, q: (rbq + b * nq + q, OFF_AQ // 512 + g)),
        pl.BlockSpec((seq, 128), lambda b, g, q: (rbs + b, OFF_AK // 128 + g)),
        pl.BlockSpec((seq, 128), lambda b, g, q: (rbs + b, OFF_AV // 128 + g)),
        pl.BlockSpec((None, 1, ATT_D), lambda b, g, q: (layer, 0, 0)),
        pl.BlockSpec((None, 1, ATT_D), lambda b, g, q: (layer, 0, 0)),
    ]
    args = [proj, proj, proj, qg, kg]
    if latent:
        cs = pl.BlockSpec((None, None, None, past, ATT_D), lambda b, g, q: (b, layer, g, 0, 0))
        ts = pl.BlockSpec((seq, ATT_D), lambda b, g, q: (0, 0))
        in_specs += [cs, cs, ts, ts, ts]
        args += [ctx[0], ctx[1], rope[0], rope[1], rope[2]]
    out_specs = [pl.BlockSpec((tq, 512), lambda b, g, q: (b * nq + q, g))]
    out_shape = [jax.ShapeDtypeStruct((nb * seq, D), BF16)]
    if not latent:
        so = pl.BlockSpec((None, None, seq, ATT_D), lambda b, g, q: (b, g, 0, 0))
        out_specs += [so, so]
        out_shape += [jax.ShapeDtypeStruct((nb, ATT_G, seq, ATT_D), F32)] * 2
    return pl.pallas_call(
        functools.partial(_att_kernel, seq=seq, past=past, tq=tq, latent=latent),
        grid=(nb, ATT_G, nq),
        in_specs=in_specs,
        out_specs=out_specs,
        out_shape=out_shape,
        scratch_shapes=[pltpu.VMEM((past + seq, ATT_D), BF16), pltpu.VMEM((past + seq, ATT_D), BF16)],
        compiler_params=_cp(("parallel", "parallel", "arbitrary")),
        name="att_latent" if latent else "att_ctx",
    )(*args)


def _ret_kernel(*refs, seq, latent, out_state):
    it = iter(refs)
    q_ref, k_ref, v_ref, rg_ref, rd_ref, ng_ref = (next(it) for _ in range(6))
    if latent:
        cos_ref, sa_ref, sb_ref, s0f_ref, s0b_ref = (next(it) for _ in range(5))
    o_ref = next(it)
    if out_state:
        sout_refs = (next(it), next(it))
    of_ref, st_ref = next(it), next(it)

    nc = seq // CH
    ri = lax.broadcasted_iota(jnp.int32, (CH, CH), 0)
    ci = lax.broadcasted_iota(jnp.int32, (CH, CH), 1)
    pos = ri.astype(F32)
    diff = (ri - ci).astype(F32)
    scale = RET_K ** -0.5
    rd = rd_ref[...]
    lg_all = jnp.minimum(rd, 0.0) - jnp.log(1.0 + jnp.exp(-jnp.abs(rd)))

    for d in range(2):
        lg = lg_all[d:d + 1, :]
        if d == 0:
            dm = jnp.exp(jnp.where(ri >= ci, diff * lg, -jnp.inf))
            kte = jnp.exp((CH - 1.0 - pos) * lg)
            qfs = jnp.exp((pos + 1.0) * lg)
        else:
            dm = jnp.exp(jnp.where(ci > ri, -diff * lg, -jnp.inf))
            kte = jnp.exp(pos * lg)
            qfs = jnp.exp((CH - pos) * lg)
        cdec = jnp.exp(CH * lg)
        cdec2 = jnp.concatenate([cdec, cdec], axis=1)
        if latent:
            st_ref[...] = (s0f_ref if d == 0 else s0b_ref)[...]
        else:
            st_ref[...] = jnp.zeros_like(st_ref)

        def chunk(c, carry, d=d, dm=dm, kte=kte, qfs=qfs, cdec2=cdec2):
            ch = c if d == 0 else nc - 1 - c
            t0 = pl.multiple_of(ch * CH, CH)
            q = q_ref[pl.ds(t0, CH), :] * scale
            k = k_ref[pl.ds(t0, CH), :]
            if latent:
                cos = cos_ref[pl.ds(t0, CH), :]
                sa = sa_ref[pl.ds(t0, CH), :]
                sb = sb_ref[pl.ds(t0, CH), :]
                q = _rope(q, cos, sa, sb)
                k = _rope(k, cos, sa, sb)
            vb = v_ref[pl.ds(t0, CH), :].astype(BF16)
            sc = _dot_nt(q.astype(BF16), k.astype(BF16)) * dm
            st = st_ref[...]
            o = _dot(sc.astype(BF16), vb) + _dot((q * qfs).astype(BF16), st.astype(BF16))
            st_ref[...] = cdec2 * st + _dot((k * kte).T.astype(BF16), vb)
            if d == 0:
                of_ref[pl.ds(t0, CH), :] = o
            else:
                tot = of_ref[pl.ds(t0, CH), :] + o
                y = _silu(rg_ref[pl.ds(t0, CH), :]) * _rms(tot, ng_ref[...])
                o_ref[pl.ds(t0, CH), :] = y.astype(BF16)
            return carry

        lax.fori_loop(0, nc, chunk, 0, unroll=2)
        if out_state:
            sout_refs[d][...] = st_ref[...]


def _ret(proj, row_off, nb, seq, rd, ng, layer, rope=None, s0=None, out_state=False):
    latent = s0 is not None
    rb = row_off // seq
    in_specs = [
        pl.BlockSpec((seq, RET_K), lambda b, h: (rb + b, OFF_RQ // RET_K + h)),
        pl.BlockSpec((seq, RET_K), lambda b, h: (rb + b, OFF_RK // RET_K + h)),
        pl.BlockSpec((seq, RET_V), lambda b, h: (rb + b, OFF_RV // RET_V + h)),
        pl.BlockSpec((seq, RET_V), lambda b, h: (rb + b, OFF_RG // RET_V + h)),
        pl.BlockSpec((None, None, 2, 128), lambda b, h: (layer, h, 0, 0)),
        pl.BlockSpec((None, 1, RET_V), lambda b, h: (layer, 0, h)),
    ]
    args = [proj, proj, proj, proj, rd, ng]
    if latent:
        ts = pl.BlockSpec((seq, RET_K), lambda b, h: (0, 0))
        ss = pl.BlockSpec((None, None, None, RET_K, RET_V), lambda b, h: (b, layer, h, 0, 0))
        in_specs += [ts, ts, ts, ss, ss]
        args += [rope[0], rope[1], rope[2], s0[0], s0[1]]
    out_specs = [pl.BlockSpec((seq, RET_V), lambda b, h: (b, h))]
    out_shape = [jax.ShapeDtypeStruct((nb * seq, D), BF16)]
    if out_state:
        so = pl.BlockSpec((None, None, RET_K, RET_V), lambda b, h: (b, h, 0, 0))
        out_specs += [so, so]
        out_shape += [jax.ShapeDtypeStruct((nb, RET_H, RET_K, RET_V), F32)] * 2
    return pl.pallas_call(
        functools.partial(_ret_kernel, seq=seq, latent=latent, out_state=out_state),
        grid=(nb, RET_H),
        in_specs=in_specs,
        out_specs=out_specs,
        out_shape=out_shape,
        scratch_shapes=[pltpu.VMEM((seq, RET_V), F32), pltpu.VMEM((RET_K, RET_V), F32)],
        compiler_params=_cp(("parallel", "parallel")),
        name="ret_latent" if latent else "ret_ctx",
    )(*args)


def _merge_kernel(ys_ref, z_ref, ng_ref, ya_ref, yr_ref, g0_ref, g1_ref, g2_ref, wb_ref, o_ref, y0_ref):
    @pl.when(pl.program_id(1) == 0)
    def _():
        y = ys_ref[...] * _silu(z_ref[...])
        y0_ref[...] = _rms(y, ng_ref[...]).astype(BF16)

    acc = _sigmoid(g0_ref[...]) * _dot(y0_ref[...], wb_ref[0])
    acc += _sigmoid(g1_ref[...]) * _dot(ya_ref[...], wb_ref[1])
    acc += _sigmoid(g2_ref[...]) * _dot(yr_ref[...], wb_ref[2])
    o_ref[...] = acc.astype(BF16)


def _merge(y_ssd, proj, ssd_ng, y_att, y_ret, w_branch, layer, tm=512, tn=512):
    t = y_ssd.shape[0]
    gl = OFF_GL // tn
    nj = D // tn
    return pl.pallas_call(
        _merge_kernel,
        grid=(t // tm, nj),
        in_specs=[pl.BlockSpec((tm, D), lambda i, j: (i, 0)),
                  pl.BlockSpec((tm, D), lambda i, j: (i, OFF_Z // D)),
                  pl.BlockSpec((None, 1, D), lambda i, j: (layer, 0, 0)),
                  pl.BlockSpec((tm, D), lambda i, j: (i, 0)),
                  pl.BlockSpec((tm, D), lambda i, j: (i, 0)),
                  pl.BlockSpec((tm, tn), lambda i, j: (i, gl + j)),
                  pl.BlockSpec((tm, tn), lambda i, j: (i, gl + nj + j)),
                  pl.BlockSpec((tm, tn), lambda i, j: (i, gl + 2 * nj + j)),
                  pl.BlockSpec((None, 3, D, tn), lambda i, j: (layer, 0, 0, j))],
        out_specs=pl.BlockSpec((tm, tn), lambda i, j: (i, j)),
        out_shape=jax.ShapeDtypeStruct((t, D), BF16),
        scratch_shapes=[pltpu.VMEM((tm, D), BF16)],
        compiler_params=_cp(("parallel", "arbitrary")),
        name="merge",
    )(y_ssd, proj, ssd_ng, y_att, y_ret, proj, proj, proj, w_branch)


def _outproj_kernel(m_ref, w_ref, x_ref, g1_ref, ng_ref, sh_ref, sc_ref, wr_ref,
                    x1_ref, h2_ref, aff_ref, afft_ref):
    x1 = x_ref[...] + g1_ref[...] * _dot(m_ref[...], w_ref[...])
    x1_ref[...] = x1
    h = _rms(x1, ng_ref[...]) * (1.0 + sc_ref[...]) + sh_ref[...]
    hb = h.astype(BF16)
    h2_ref[...] = hb
    hl = (h - hb.astype(F32)).astype(BF16)
    logits = _dot(hb, wr_ref[0]) + _dot(hl, wr_ref[0]) + _dot(hb, wr_ref[1])
    lane = lax.broadcasted_iota(jnp.int32, logits.shape, 1)
    logits = jnp.where(lane < N_EXP, logits, -jnp.inf)
    e = jnp.exp(logits - jnp.max(logits, axis=-1, keepdims=True))
    aff = e / jnp.sum(e, axis=-1, keepdims=True)
    aff_ref[...] = aff
    afft_ref[...] = aff.T[0:N_EXP, :]


def _outproj(merged, w_out, x, mods, norm_g, w_router, layer, mod_row, tm=512):
    t = x.shape[0]

    def ms(col):
        return pl.BlockSpec((None, None, 1, D), lambda i: (layer, mod_row(i, tm), 0, col))

    return pl.pallas_call(
        _outproj_kernel,
        grid=(t // tm,),
        in_specs=[pl.BlockSpec((tm, D), lambda i: (i, 0)),
                  pl.BlockSpec((None, D, D), lambda i: (layer, 0, 0)),
                  pl.BlockSpec((tm, D), lambda i: (i, 0)),
                  ms(2),
                  pl.BlockSpec((None, 1, D), lambda i: (layer, 0, 0)),
                  ms(3), ms(4),
                  pl.BlockSpec((None, 2, D, 128), lambda i: (layer, 0, 0, 0))],
        out_specs=[pl.BlockSpec((tm, D), lambda i: (i, 0)),
                   pl.BlockSpec((tm, D), lambda i: (i, 0)),
                   pl.BlockSpec((tm, 128), lambda i: (i, 0)),
                   pl.BlockSpec((N_EXP, tm), lambda i: (0, i))],
        out_shape=[jax.ShapeDtypeStruct((t, D), F32),
                   jax.ShapeDtypeStruct((t, D), BF16),
                   jax.ShapeDtypeStruct((t, 128), F32),
                   jax.ShapeDtypeStruct((N_EXP, t), F32)],
        compiler_params=_cp(("parallel",)),
        name="out_proj",
    )(merged, w_out, x, mods, norm_g, mods, mods, w_router)


SUB = 128
WIN = SUB + 16


def _select_kernel(a_ref, post_ref, postok_ref, st_ref, cnt_ref, *, groups):
    ri = lax.broadcasted_iota(jnp.int32, (SUB, SUB), 0)
    ci = lax.broadcasted_iota(jnp.int32, (SUB, SUB), 1)
    before = jnp.where(ri < ci, 1.0, 0.0).astype(BF16)
    lane = lax.broadcasted_iota(jnp.int32, (N_EXP, SUB), 1)
    st_ref[...] = jnp.zeros_like(st_ref)
    for lo, n, cap, base in groups:
        bits = pltpu.bitcast(a_ref[:, lo:lo + n], jnp.int32)

        def search(i, prefix, bits=bits, cap=cap):
            cand = prefix | jnp.left_shift(jnp.int32(1), 30 - i)
            cnt = jnp.sum(jnp.where(bits >= cand, 1.0, 0.0), axis=1, keepdims=True)
            return jnp.where(cnt >= cap, cand, prefix)

        zero = pltpu.bitcast(jnp.minimum(jnp.min(a_ref[:, lo:lo + n], axis=1, keepdims=True), 0.0), jnp.int32)
        tau = lax.fori_loop(0, 31, search, zero)
        need = cap - jnp.sum(jnp.where(bits > tau, 1.0, 0.0), axis=1, keepdims=True)

        cnt_ref[...] = jnp.zeros_like(cnt_ref)

        def block(b, carry, lo=lo, base=base, tau=tau, need=need):
            ceq = cnt_ref[0][:, 0:1]
            csel = cnt_ref[1][:, 0:1]
            t0 = pl.multiple_of(lo + b * SUB, SUB)
            bb = pltpu.bitcast(a_ref[:, pl.ds(t0, SUB)], jnp.int32)
            eq = jnp.where(bb == tau, 1.0, 0.0)
            rank_eq = ceq + _dot(eq.astype(BF16), before)
            sel = jnp.where((bb > tau) | ((bb == tau) & (rank_eq < need)), 1.0, 0.0)
            pos = csel + _dot(sel.astype(BF16), before) + base
            pm = jnp.where(sel > 0.0, pos, -1.0)
            post_ref[:, pl.ds(t0, SUB)] = pm
            full = jnp.concatenate([pm, jnp.full((SUB - N_EXP, SUB), -1.0, F32)], axis=0)
            postok_ref[pl.ds(t0, SUB), :] = full.T
            st_ref[...] = jnp.where(lane == lo // SUB + b, (csel + base).astype(jnp.int32), st_ref[...])
            cnt_ref[0] = jnp.broadcast_to(ceq + jnp.sum(eq, axis=1, keepdims=True), (N_EXP, SUB))
            cnt_ref[1] = jnp.broadcast_to(csel + jnp.sum(sel, axis=1, keepdims=True), (N_EXP, SUB))
            return carry

        lax.fori_loop(0, n // SUB, block, 0)


def _select(afft, groups):
    t = afft.shape[1]
    assert t // SUB <= SUB
    return pl.pallas_call(
        functools.partial(_select_kernel, groups=groups),
        out_shape=[jax.ShapeDtypeStruct((N_EXP, t), F32),
                   jax.ShapeDtypeStruct((t, SUB), F32),
                   jax.ShapeDtypeStruct((N_EXP, SUB), jnp.int32)],
        scratch_shapes=[pltpu.VMEM((2, N_EXP, SUB), F32)],
        compiler_params=pltpu.CompilerParams(vmem_limit_bytes=VMEM_LIMIT),
        name="select",
    )(afft)


def _ffn_kernel(st_ref, h_ref, pos_ref, w1_ref, w3_ref, w2_ref, o_ref, stage_ref, acc_ref, *, nbd, cap, tbd):
    e = pl.program_id(0)
    s = pl.program_id(1)
    nf = pl.num_programs(1) - nbd

    @pl.when(s == 0)
    def _():
        stage_ref[...] = jnp.zeros_like(stage_ref)

    @pl.when(s < nbd)
    def _():
        srow = lax.broadcasted_iota(jnp.int32, (WIN, SUB), 0).astype(F32)
        for k in range(tbd // SUB):
            off16 = pl.multiple_of((st_ref[e, s * (tbd // SUB) + k] // 16) * 16, 16)
            rel = pos_ref[:, k * SUB:(k + 1) * SUB] - off16.astype(F32)
            onehot = jnp.where(srow == rel, 1.0, 0.0).astype(BF16)
            rows = _dot(onehot, h_ref[k * SUB:(k + 1) * SUB, :]).astype(BF16)
            stage_ref[pl.ds(off16, WIN), :] += rows

    @pl.when(s >= nbd)
    def _():
        f = s - nbd
        x = stage_ref[0:cap, :]
        hid = _silu(_dot(x, w1_ref[...].astype(BF16))) * _dot(x, w3_ref[...].astype(BF16))
        contrib = _dot(hid.astype(BF16), w2_ref[...].astype(BF16))

        @pl.when(f == 0)
        def _():
            acc_ref[...] = contrib

        @pl.when((f > 0) & (f < nf - 1))
        def _():
            acc_ref[...] += contrib

        @pl.when(f == nf - 1)
        def _():
            o_ref[...] = (acc_ref[...] + contrib).astype(BF16)


def _ffn(starts, h2, post, w1, w3, w2, layer, cap, tbd=512, tf=256):
    t = h2.shape[0]
    n_e, ff = w1.shape[1], w1.shape[-1]
    nbd = t // tbd
    nf = ff // tf
    assert nf >= 2

    def tok(s):
        return jnp.minimum(s, nbd - 1)

    def fcol(s):
        return jnp.maximum(s - nbd, 0)

    grid_spec = pltpu.PrefetchScalarGridSpec(
        num_scalar_prefetch=1,
        grid=(n_e, nbd + nf),
        in_specs=[pl.BlockSpec((tbd, D), lambda e, s, st: (tok(s), 0)),
                  pl.BlockSpec((None, 1, tbd), lambda e, s, st: (e, 0, tok(s))),
                  pl.BlockSpec((None, None, D, tf), lambda e, s, st: (layer, e, 0, fcol(s))),
                  pl.BlockSpec((None, None, D, tf), lambda e, s, st: (layer, e, 0, fcol(s))),
                  pl.BlockSpec((None, None, tf, D), lambda e, s, st: (layer, e, fcol(s), 0))],
        out_specs=pl.BlockSpec((None, cap, D), lambda e, s, st: (e, 0, 0)),
        scratch_shapes=[pltpu.VMEM((cap + WIN, D), BF16), pltpu.VMEM((cap, D), F32)],
    )
    return pl.pallas_call(
        functools.partial(_ffn_kernel, nbd=nbd, cap=cap, tbd=tbd),
        grid_spec=grid_spec,
        out_shape=jax.ShapeDtypeStruct((n_e, cap, D), BF16),
        compiler_params=pltpu.CompilerParams(dimension_semantics=("parallel", "arbitrary"),
                                             vmem_limit_bytes=60 * 1024 * 1024),
        name="expert_ffn",
    )(starts, h2, post.reshape(n_e, 1, t), w1, w3, w2)


def _combine_kernel(st_ref, ye_hbm, postok_ref, aff_ref, x1_ref, g2_ref, o_ref, win_ref, sem, *, cap):
    b = pl.program_id(0)
    nb = pl.num_programs(0)

    def first_slot(bb, e):
        return pl.multiple_of(jnp.minimum((st_ref[e, bb] // 16) * 16, cap - WIN), 16)

    def window_copy(bb, slot, e):
        return pltpu.make_async_copy(ye_hbm.at[e, pl.ds(first_slot(bb, e), WIN), :],
                                     win_ref.at[slot, pl.ds(e * WIN, WIN), :], sem.at[slot])

    @pl.when(b == 0)
    def _():
        tail = jnp.zeros((2 * SUB - WIN, D), BF16)
        win_ref[0, N_EXP * WIN:, :] = tail
        win_ref[1, N_EXP * WIN:, :] = tail
        for e in range(N_EXP):
            window_copy(0, 0, e).start()

    @pl.when(b + 1 < nb)
    def _():
        for e in range(N_EXP):
            window_copy(b + 1, (b + 1) % 2, e).start()

    slot = b % 2
    for e in range(N_EXP):
        window_copy(b, slot, e).wait()

    lane = lax.broadcasted_iota(jnp.int32, (SUB, 2 * SUB), 1).astype(F32)
    acc = jnp.zeros((SUB, D), F32)
    for e in range(N_EXP):
        rel = postok_ref[:, e:e + 1] - first_slot(b, e).astype(F32)
        gated = jnp.where(lane == rel, aff_ref[:, e:e + 1], 0.0).astype(BF16)
        acc += _dot(gated, win_ref[slot, pl.ds(e * WIN, 2 * SUB), :])
    o_ref[...] = x1_ref[...] + g2_ref[...] * acc


def _combine(starts, ye, postok, aff, x1, mods, layer, mod_row):
    t = x1.shape[0]
    n_e, cap, _ = ye.shape
    grid_spec = pltpu.PrefetchScalarGridSpec(
        num_scalar_prefetch=1,
        grid=(t // SUB,),
        in_specs=[pl.BlockSpec(memory_space=pl.ANY),
                  pl.BlockSpec((SUB, SUB), lambda b, st: (b, 0)),
                  pl.BlockSpec((SUB, SUB), lambda b, st: (b, 0)),
                  pl.BlockSpec((SUB, D), lambda b, st: (b, 0)),
                  pl.BlockSpec((None, None, 1, D), lambda b, st: (layer, mod_row(b, SUB), 0, 5))],
        out_specs=pl.BlockSpec((SUB, D), lambda b, st: (b, 0)),
        scratch_shapes=[pltpu.VMEM((2, N_EXP * WIN + 2 * SUB - WIN, D), BF16),
                        pltpu.SemaphoreType.DMA((2,))],
    )
    return pl.pallas_call(
        functools.partial(_combine_kernel, cap=cap),
        grid_spec=grid_spec,
        out_shape=jax.ShapeDtypeStruct((t, D), F32),
        compiler_params=_cp(("arbitrary",)),
        name="combine",
    )(starts, ye, postok, aff, x1, mods)


def _rope_tables(n_tokens):
    rows_n = n_tokens // GRID_W
    row = jnp.repeat(jnp.arange(rows_n, dtype=F32), GRID_W)
    col = jnp.tile(jnp.arange(GRID_W, dtype=F32), rows_n)
    n_freq = ATT_D // 4
    inv = ROPE_THETA ** (-jnp.arange(n_freq, dtype=F32) / n_freq)
    ang_r = row[:, None] * inv
    ang_c = col[:, None] * inv
    zeros = jnp.zeros_like(ang_r)
    cos = jnp.concatenate([jnp.cos(ang_r)] * 2 + [jnp.cos(ang_c)] * 2, axis=1)
    sa = jnp.concatenate([-jnp.sin(ang_r), zeros, -jnp.sin(ang_c), zeros], axis=1)
    sb = jnp.concatenate([zeros, jnp.sin(ang_r), zeros, jnp.sin(ang_c)], axis=1)
    return cos, sa, sb


def _group_dt_lanes(v):
    depth = v.shape[0]
    v = v.reshape(depth, 2, SSD_G, SSD_R).transpose(0, 2, 1, 3).reshape(depth, SSD_G, 1, 2 * SSD_R)
    return jnp.pad(v, ((0, 0), (0, 0), (0, 0), (0, 128 - 2 * SSD_R)))


def _prep_w_in(w_in):
    depth = w_in.shape[0]
    dt = w_in[:, :, ORIG_DT:ORIG_AFTER_DT].reshape(depth, D, 2, SSD_G, SSD_R)
    dt = dt.transpose(0, 1, 3, 2, 4).reshape(depth, D, SSD_G, 2 * SSD_R)
    dt = jnp.pad(dt, ((0, 0), (0, 0), (0, 0), (0, 128 - 2 * SSD_R))).reshape(depth, D, SSD_G * 128)
    w = jnp.concatenate([w_in[:, :, :ORIG_DT], w_in[:, :, ORIG_AFTER_DT:], dt], axis=2)
    return w.astype(BF16)


def kernel(x_prompt, x_sample, cache_attn_k, cache_attn_v, state_ssd_fwd, state_ssd_bwd, state_ret_fwd, state_ret_bwd, c, c_ctx, norm1_g, norm2_g, w_ada, b_ada, w_in, ssd_conv_w, ssd_conv_b, ssd_a_log, ssd_dt_bias, ssd_d, ssd_norm_g, q_norm_g, k_norm_g, ret_decay, ret_norm_g, w_branch, w_out, w_router, w_e1, w_e3, w_e2):
    depth = w_in.shape[0]
    pb, pseq, _ = x_prompt.shape
    sb, sseq, _ = x_sample.shape
    n_p = pb * pseq
    n_s = sb * sseq
    assert n_p % 1024 == 0 and sseq % 1024 == 0 and 1 + sb <= 8

    def mod_row(i, tm):
        return jnp.where(i < n_p // tm, 0, 1 + (i * tm - n_p) // sseq)

    w_in_b = _prep_w_in(w_in)
    w_branch_b = w_branch.astype(BF16)
    w_out_b = w_out.astype(BF16)
    cap_p = CAP_FACTOR * n_p // N_EXP
    cap_s = CAP_FACTOR * n_s // N_EXP
    groups = ((0, n_p, cap_p, 0), (n_p, n_s, cap_s, cap_p))
    w_router_f = jnp.pad(w_router, ((0, 0), (0, 0), (0, 128 - N_EXP)))
    w_router_hi = w_router_f.astype(BF16)
    w_router_p = jnp.stack([w_router_hi, (w_router_f - w_router_hi.astype(F32)).astype(BF16)], axis=1)
    dtb = _group_dt_lanes(ssd_dt_bias)
    alog = _group_dt_lanes(ssd_a_log)
    dvec = jnp.repeat(ssd_d, SSD_P, axis=1).reshape(depth, 1, D)
    rd = jnp.broadcast_to(ret_decay.transpose(0, 2, 1)[..., None], (depth, RET_H, 2, 128))
    rope = _rope_tables(sseq)
    n1g = norm1_g.reshape(depth, 1, D)
    n2g = norm2_g.reshape(depth, 1, D)
    ssd_ng = ssd_norm_g.reshape(depth, 1, D)
    ret_ng = ret_norm_g.reshape(depth, 1, D)
    qg = q_norm_g.reshape(depth, 1, ATT_D)
    kg = k_norm_g.reshape(depth, 1, ATT_D)

    cond8 = jnp.zeros((8, D), F32).at[0].set(c_ctx).at[1:1 + sb].set(c)
    mods = _ada(cond8, w_ada, b_ada).reshape(depth, 8, 1, 6 * D)

    x = jnp.concatenate([x_prompt.reshape(n_p, D), x_sample.reshape(n_s, D)], axis=0)
    new_k, new_v, new_hf, new_hb, new_sf, new_sb = [], [], [], [], [], []
    for l in range(depth):
        proj = _inproj(x, n1g, mods, w_in_b, l, mod_row)

        ys_p, hf, hb = _ssd(proj, 0, pb, pseq, ssd_conv_w, ssd_conv_b.reshape(depth, 1, -1), dtb, alog, dvec,
                            l, out_state=True)
        (ys_s,) = _ssd(proj, n_p, sb, sseq, ssd_conv_w, ssd_conv_b.reshape(depth, 1, -1), dtb, alog, dvec,
                       l, h0=(state_ssd_fwd, state_ssd_bwd))
        ya_p, kn, vn = _att(proj, 0, pb, pseq, qg, kg, l, tq=pseq)
        (ya_s,) = _att(proj, n_p, sb, sseq, qg, kg, l, tq=256, ctx=(cache_attn_k, cache_attn_v), rope=rope)
        yr_p, sf, sbk = _ret(proj, 0, pb, pseq, rd, ret_ng, l, out_state=True)
        (yr_s,) = _ret(proj, n_p, sb, sseq, rd, ret_ng, l, rope=rope, s0=(state_ret_fwd, state_ret_bwd))
        new_k.append(kn), new_v.append(vn), new_hf.append(hf), new_hb.append(hb)
        new_sf.append(sf), new_sb.append(sbk)

        y_ssd = jnp.concatenate([ys_p, ys_s], axis=0)
        y_att = jnp.concatenate([ya_p, ya_s], axis=0)
        y_ret = jnp.concatenate([yr_p, yr_s], axis=0)
        merged = _merge(y_ssd, proj, ssd_ng, y_att, y_ret, w_branch_b, l)
        x1, h2, aff, afft = _outproj(merged, w_out_b, x, mods, n2g, w_router_p, l, mod_row)
        post, postok, starts = _select(afft, groups)
        ye = _ffn(starts, h2, post, w_e1, w_e3, w_e2, l, cap_p + cap_s)
        x = _combine(starts, ye, postok, aff, x1, mods, l, mod_row)

    y_prompt = x[:n_p].reshape(pb, pseq, D)
    y_sample = x[n_p:].reshape(sb, sseq, D)
    return (y_prompt, y_sample, jnp.stack(new_k, axis=1), jnp.stack(new_v, axis=1),
            jnp.stack(new_hf, axis=1), jnp.stack(new_hb, axis=1),
            jnp.stack(new_sf, axis=1), jnp.stack(new_sb, axis=1))
```

```python
import functools

import jax
import jax.numpy as jnp
from jax import lax
from jax.experimental import pallas as pl
from jax.experimental.pallas import tpu as pltpu

F32 = jnp.float32
BF16 = jnp.bfloat16
EPS = 1e-6
ROPE_THETA = 10000.0
GRID_W = 64

D = 2048
CH = 128
SSD_P = 64
SSD_N = 128
SSD_G = 4
SSD_H = 32
SSD_R = SSD_H // SSD_G
ATT_D = 128
ATT_H = 16
ATT_G = 4
ATT_R = ATT_H // ATT_G
RET_H = 8
RET_K = 128
RET_V = 256
N_EXP = 16
CAP_FACTOR = 2

OFF_Z = 0
OFF_X = 2048
OFF_B = 4096
OFF_C = 4608
OFF_AQ = 5120
OFF_AK = 7168
OFF_AV = 7680
OFF_RQ = 8192
OFF_RK = 9216
OFF_RV = 10240
OFF_RG = 12288
OFF_GL = 14336
OFF_DT = 20480
NW = OFF_DT + SSD_G * 128
ORIG_DT = 5120
ORIG_AFTER_DT = 5184
ORIG_W = 20544

VMEM_LIMIT = 56 * 1024 * 1024


def _cp(sem):
    return pltpu.CompilerParams(dimension_semantics=sem, vmem_limit_bytes=VMEM_LIMIT)


def _sigmoid(x):
    return 1.0 / (1.0 + jnp.exp(-x))


def _silu(x):
    return x * _sigmoid(x)


def _softplus(x):
    return jnp.maximum(x, 0.0) + jnp.log(1.0 + jnp.exp(-jnp.abs(x)))


def _rms(x, g):
    ms = jnp.mean(x * x, axis=-1, keepdims=True)
    return x * lax.rsqrt(ms + EPS) * g


def _dot(a, b):
    return jnp.dot(a, b, preferred_element_type=F32)


def _dot_nt(a, b):
    return lax.dot_general(a, b, (((1,), (1,)), ((), ())), preferred_element_type=F32)


def _rope(x, cos, sa, sb):
    return x * cos + pltpu.roll(x, 96, 1) * sa + pltpu.roll(x, 32, 1) * sb


def _ada_kernel(c_ref, w_ref, b_ref, o_ref):
    s = _silu(c_ref[...]).astype(BF16)
    o_ref[...] = _dot(s, w_ref[...].astype(BF16)) + b_ref[...]


def _ada(cond8, w_ada, b_ada):
    depth, _, n6 = w_ada.shape
    tn = 1024
    return pl.pallas_call(
        _ada_kernel,
        grid=(depth, n6 // tn),
        in_specs=[pl.BlockSpec((8, D), lambda l, j: (0, 0)),
                  pl.BlockSpec((None, D, tn), lambda l, j: (l, 0, j)),
                  pl.BlockSpec((None, 1, tn), lambda l, j: (l, 0, j))],
        out_specs=pl.BlockSpec((None, 8, tn), lambda l, j: (l, 0, j)),
        out_shape=jax.ShapeDtypeStruct((depth, 8, n6), F32),
        compiler_params=_cp(("parallel", "parallel")),
        name="ada_mod",
    )(cond8, w_ada, b_ada.reshape(depth, 1, n6))


def _inproj_kernel(x_ref, g_ref, sh_ref, sc_ref, w_ref, o_ref, hn_ref):
    @pl.when(pl.program_id(1) == 0)
    def _():
        h = _rms(x_ref[...], g_ref[...])
        hn_ref[...] = (h * (1.0 + sc_ref[...]) + sh_ref[...]).astype(BF16)

    o_ref[...] = _dot(hn_ref[...], w_ref[...])


def _inproj(x, norm_g, mods, w_in, layer, mod_row, tm=1024, tn=512):
    t = x.shape[0]
    return pl.pallas_call(
        _inproj_kernel,
        grid=(t // tm, NW // tn),
        in_specs=[pl.BlockSpec((tm, D), lambda i, j: (i, 0)),
                  pl.BlockSpec((None, 1, D), lambda i, j: (layer, 0, 0)),
                  pl.BlockSpec((None, None, 1, D), lambda i, j: (layer, mod_row(i, tm), 0, 0)),
                  pl.BlockSpec((None, None, 1, D), lambda i, j: (layer, mod_row(i, tm), 0, 1)),
                  pl.BlockSpec((None, D, tn), lambda i, j: (layer, 0, j))],
        out_specs=pl.BlockSpec((tm, tn), lambda i, j: (i, j)),
        out_shape=jax.ShapeDtypeStruct((t, NW), F32),
        scratch_shapes=[pltpu.VMEM((tm, D), BF16)],
        compiler_params=_cp(("parallel", "arbitrary")),
        name="in_proj",
    )(x, norm_g, mods, mods, w_in)


def _ssd_kernel(*refs, seq, has_h0, out_state, has_dst):
    it = iter(refs)
    x_ref, b_ref, c_ref, dt_ref = next(it), next(it), next(it), next(it)
    cwx_ref, cwb_ref, cwc_ref = next(it), next(it), next(it)
    cbx_ref, cbb_ref, cbc_ref = next(it), next(it), next(it)
    dtb_ref, alog_ref, dvec_ref = next(it), next(it), next(it)
    h0_refs = (next(it), next(it)) if has_h0 else None
    if has_dst:
        next(it)
    y_ref = next(it)
    hout_refs = (next(it), next(it)) if out_state else None
    xs_ref, bs_ref, cs_ref, st_ref = next(it), next(it), next(it), next(it)

    nc = seq // CH
    row1 = lax.broadcasted_iota(jnp.int32, (CH, 1), 0)

    def conv_chunk(c, carry):
        t0 = pl.multiple_of(c * CH, CH)
        pidx = pl.multiple_of(jnp.maximum(t0 - 8, 0), 8)
        nidx = pl.multiple_of(jnp.minimum(t0 + CH, seq - 8), 8)

        def conv(src, w_ref, bias_ref, dst):
            cur = src[pl.ds(t0, CH), :]
            prev_row = jnp.where(c > 0, src[pl.ds(pidx, 8), :][7:8, :], 0.0)
            next_row = jnp.where(c < nc - 1, src[pl.ds(nidx, 8), :][0:1, :], 0.0)
            xm = jnp.where(row1 == 0, prev_row, pltpu.roll(cur, 1, 0))
            xp = jnp.where(row1 == CH - 1, next_row, pltpu.roll(cur, CH - 1, 0))
            w = w_ref[...]
            v = xm * w[0:1, :] + cur * w[1:2, :] + xp * w[2:3, :] + bias_ref[...]
            v = _silu(v)
            dst[pl.ds(t0, CH), :] = v
            return v

        xs = conv(x_ref, cwx_ref, cbx_ref, xs_ref)
        y_ref[pl.ds(t0, CH), :] = dvec_ref[...] * xs
        conv(b_ref, cwb_ref, cbb_ref, bs_ref)
        conv(c_ref, cwc_ref, cbc_ref, cs_ref)
        return carry

    lax.fori_loop(0, nc, conv_chunk, 0)

    ri = lax.broadcasted_iota(jnp.int32, (CH, CH), 0)
    ci = lax.broadcasted_iota(jnp.int32, (CH, CH), 1)
    tri = (ri >= ci).astype(F32)
    left = ci < SSD_P
    left_row = ci[0:1, :] < SSD_P
    top = row1 < SSD_P
    dtb = dtb_ref[...]
    aneg = -jnp.exp(alog_ref[...])

    for d in range(2):
        if has_h0:
            st_ref[d] = h0_refs[d][...].reshape(SSD_R // 2, 2 * SSD_P, SSD_N)
        else:
            st_ref[d] = jnp.zeros((SSD_R // 2, 2 * SSD_P, SSD_N), F32)

    def scan_one(d, ch):
        mask = (ri >= ci) if d == 0 else (ci >= ri)
        t0 = pl.multiple_of(ch * CH, CH)
        dtv = _softplus(dt_ref[pl.ds(t0, CH), :] + dtb)
        a = dtv * aneg
        pref = jnp.dot(tri, a, precision=lax.Precision.HIGHEST, preferred_element_type=F32)
        tot = pref[CH - 1:CH, :]
        s = pref if d == 0 else tot - pref + a
        cd = jnp.exp(tot)
        s_t = s.T
        bcb = bs_ref[pl.ds(t0, CH), :].astype(BF16)
        ccb = cs_ref[pl.ds(t0, CH), :].astype(BF16)
        cb = _dot_nt(ccb, bcb)
        for pair in range(SSD_R // 2):
            la = d * SSD_R + 2 * pair
            lb = la + 1
            x2 = xs_ref[pl.ds(t0, CH), pair * 128:(pair + 1) * 128]
            sa = jnp.broadcast_to(s[:, la:la + 1], (CH, CH))
            sb = jnp.broadcast_to(s[:, lb:lb + 1], (CH, CH))
            w_a = (cb * jnp.exp(jnp.where(mask, sa - s_t[la:la + 1, :], -jnp.inf))).astype(BF16)
            w_b = (cb * jnp.exp(jnp.where(mask, sb - s_t[lb:lb + 1, :], -jnp.inf))).astype(BF16)
            s2 = jnp.where(left, sa, sb)
            tot2 = jnp.where(left_row, tot[:, la:la + 1], tot[:, lb:lb + 1])
            xdt = x2 * jnp.where(left, dtv[:, la:la + 1], dtv[:, lb:lb + 1])
            x_a = jnp.where(left, xdt, 0.0).astype(BF16)
            x_b = jnp.where(left, 0.0, xdt).astype(BF16)
            st = st_ref[d, pair]
            y = _dot(w_a, x_a) + _dot(w_b, x_b) + _dot_nt(ccb, st.astype(BF16)) * jnp.exp(s2)
            xt = (xdt * jnp.exp(tot2 - s2)).T.astype(BF16)
            cd2 = jnp.where(top, cd[:, la:la + 1], cd[:, lb:lb + 1])
            st_ref[d, pair] = cd2 * st + _dot(xt, bcb)
            y_ref[pl.ds(t0, CH), pair * 128:(pair + 1) * 128] += y

    def scan_chunk(c, carry):
        scan_one(0, c)
        scan_one(1, nc - 1 - c)
        return carry

    lax.fori_loop(0, nc, scan_chunk, 0)
    if out_state:
        for d in range(2):
            hout_refs[d][...] = st_ref[d].reshape(SSD_R, SSD_P, SSD_N)


def _ssd(proj, row_off, nb, seq, conv_w, conv_b, dtb, alog, dvec, layer, h0=None, out_state=False, dst=None):
    rb = row_off // seq
    has_h0 = h0 is not None
    in_specs = [
        pl.BlockSpec((seq, 512), lambda b, g: (rb + b, OFF_X // 512 + g)),
        pl.BlockSpec((seq, 128), lambda b, g: (rb + b, OFF_B // 128 + g)),
        pl.BlockSpec((seq, 128), lambda b, g: (rb + b, OFF_C // 128 + g)),
        pl.BlockSpec((seq, 128), lambda b, g: (rb + b, OFF_DT // 128 + g)),
        pl.BlockSpec((None, 3, 512), lambda b, g: (layer, 0, g)),
        pl.BlockSpec((None, 3, 128), lambda b, g: (layer, 0, D // 128 + g)),
        pl.BlockSpec((None, 3, 128), lambda b, g: (layer, 0, D // 128 + SSD_G + g)),
        pl.BlockSpec((None, 1, 512), lambda b, g: (layer, 0, g)),
        pl.BlockSpec((None, 1, 128), lambda b, g: (layer, 0, D // 128 + g)),
        pl.BlockSpec((None, 1, 128), lambda b, g: (layer, 0, D // 128 + SSD_G + g)),
        pl.BlockSpec((None, None, 1, 128), lambda b, g: (layer, g, 0, 0)),
        pl.BlockSpec((None, None, 1, 128), lambda b, g: (layer, g, 0, 0)),
        pl.BlockSpec((None, 1, 512), lambda b, g: (layer, 0, g)),
    ]
    args = [proj, proj, proj, proj, conv_w, conv_w, conv_w, conv_b, conv_b, conv_b, dtb, alog, dvec]
    if has_h0:
        st_spec = pl.BlockSpec((None, None, SSD_R, SSD_P, SSD_N), lambda b, g: (b, layer, g, 0, 0))
        in_specs += [st_spec, st_spec]
        args += [h0[0], h0[1]]
    aliases = {}
    if dst is not None:
        in_specs.append(pl.BlockSpec(memory_space=pl.ANY))
        args.append(dst)
        aliases = {len(args) - 1: 0}
    out_specs = [pl.BlockSpec((seq, 512), lambda b, g: (rb + b, g))]
    out_shape = [jax.ShapeDtypeStruct((proj.shape[0], D), F32)]
    if out_state:
        so = pl.BlockSpec((None, SSD_R, SSD_P, SSD_N), lambda b, g: (b, g, 0, 0))
        out_specs += [so, so]
        out_shape += [jax.ShapeDtypeStruct((nb, SSD_H, SSD_P, SSD_N), F32)] * 2
    return pl.pallas_call(
        functools.partial(_ssd_kernel, seq=seq, has_h0=has_h0, out_state=out_state, has_dst=dst is not None),
        input_output_aliases=aliases,
        grid=(nb, SSD_G),
        in_specs=in_specs,
        out_specs=out_specs,
        out_shape=out_shape,
        scratch_shapes=[pltpu.VMEM((seq, 512), F32), pltpu.VMEM((seq, 128), F32),
                        pltpu.VMEM((seq, 128), F32), pltpu.VMEM((2, SSD_R // 2, 128, 128), F32)],
        compiler_params=_cp(("parallel", "parallel")),
        name="ssd_latent" if has_h0 else "ssd_ctx",
    )(*args)


def _att_kernel(*refs, seq, past, tq, latent, has_dst):
    it = iter(refs)
    q_ref, k_ref, v_ref, qg_ref, kg_ref = next(it), next(it), next(it), next(it), next(it)
    if latent:
        kc_ref, vc_ref, cos_ref, sa_ref, sb_ref = next(it), next(it), next(it), next(it), next(it)
    if has_dst:
        next(it)
    o_ref = next(it)
    if not latent:
        ko_ref, vo_ref = next(it), next(it)
    kb_ref, vb_ref = next(it), next(it)
    qi = pl.program_id(2)

    @pl.when(qi == 0)
    def _():
        kn = _rms(k_ref[...], kg_ref[...])
        v = v_ref[...]
        if latent:
            kn = _rope(kn, cos_ref[...], sa_ref[...], sb_ref[...])
            kb_ref[0:past, :] = kc_ref[...].astype(BF16)
            vb_ref[0:past, :] = vc_ref[...].astype(BF16)
            kb_ref[past:past + seq, :] = kn.astype(BF16)
            vb_ref[past:past + seq, :] = v.astype(BF16)
        else:
            ko_ref[...] = kn
            vo_ref[...] = v
            kb_ref[...] = kn.astype(BF16)
            vb_ref[...] = v.astype(BF16)

    scale = ATT_D ** -0.5 * 1.4426950408889634
    kb = kb_ref[...]
    vb = vb_ref[...]
    if latent:
        q0 = pl.multiple_of(qi * tq, tq)
        cos = cos_ref[pl.ds(q0, tq), :]
        sa = sa_ref[pl.ds(q0, tq), :]
        sb = sb_ref[pl.ds(q0, tq), :]
    for r in range(ATT_R):
        qn = _rms(q_ref[:, r * ATT_D:(r + 1) * ATT_D], qg_ref[...])
        if latent:
            qn = _rope(qn, cos, sa, sb)
        s = _dot_nt((qn * scale).astype(BF16), kb)
        m = jnp.max(s, axis=-1, keepdims=True)
        p = jnp.exp2(s - m)
        l = jnp.sum(p, axis=-1, keepdims=True)
        o = _dot(p.astype(BF16), vb) / l
        o_ref[:, r * ATT_D:(r + 1) * ATT_D] = o.astype(BF16)


def _att(proj, row_off, nb, seq, qg, kg, layer, tq, ctx=None, rope=None, dst=None):
    latent = ctx is not None
    past = ctx[0].shape[3] if latent else 0
    nq = seq // tq
    rbq = row_off // tq
    rbs = row_off // seq
    in_specs = [
        pl.BlockSpec((tq, 512), lambda b, g, q: (rbq + b * nq + q, OFF_AQ // 512 + g)),
        pl.BlockSpec((seq, 128), lambda b, g, q: (rbs + b, OFF_AK // 128 + g)),
        pl.BlockSpec((seq, 128), lambda b, g, q: (rbs + b, OFF_AV // 128 + g)),
        pl.BlockSpec((None, 1, ATT_D), lambda b, g, q: (layer, 0, 0)),
        pl.BlockSpec((None, 1, ATT_D), lambda b, g, q: (layer, 0, 0)),
    ]
    args = [proj, proj, proj, qg, kg]
    if latent:
        cs = pl.BlockSpec((None, None, None, past, ATT_D), lambda b, g, q: (b, layer, g, 0, 0))
        ts = pl.BlockSpec((seq, ATT_D), lambda b, g, q: (0, 0))
        in_specs += [cs, cs, ts, ts, ts]
        args += [ctx[0], ctx[1], rope[0], rope[1], rope[2]]
    aliases = {}
    if dst is not None:
        in_specs.append(pl.BlockSpec(memory_space=pl.ANY))
        args.append(dst)
        aliases = {len(args) - 1: 0}
    out_specs = [pl.BlockSpec((tq, 512), lambda b, g, q: (rbq + b * nq + q, g))]
    out_shape = [jax.ShapeDtypeStruct((proj.shape[0], D), BF16)]
    if not latent:
        so = pl.BlockSpec((None, None, seq, ATT_D), lambda b, g, q: (b, g, 0, 0))
        out_specs += [so, so]
        out_shape += [jax.ShapeDtypeStruct((nb, ATT_G, seq, ATT_D), F32)] * 2
    return pl.pallas_call(
        functools.partial(_att_kernel, seq=seq, past=past, tq=tq, latent=latent, has_dst=dst is not None),
        input_output_aliases=aliases,
        grid=(nb, ATT_G, nq),
        in_specs=in_specs,
        out_specs=out_specs,
        out_shape=out_shape,
        scratch_shapes=[pltpu.VMEM((past + seq, ATT_D), BF16), pltpu.VMEM((past + seq, ATT_D), BF16)],
        compiler_params=_cp(("parallel", "parallel", "arbitrary")),
        name="att_latent" if latent else "att_ctx",
    )(*args)


def _ret_kernel(*refs, seq, latent, out_state, has_dst):
    it = iter(refs)
    q_ref, k_ref, v_ref, rg_ref, rd_ref, ng_ref = (next(it) for _ in range(6))
    if latent:
        cos_ref, sa_ref, sb_ref, s0f_ref, s0b_ref = (next(it) for _ in range(5))
    if has_dst:
        next(it)
    o_ref = next(it)
    if out_state:
        sout_refs = (next(it), next(it))
    of_ref, st_ref = next(it), next(it)

    nc = seq // CH
    ri = lax.broadcasted_iota(jnp.int32, (CH, CH), 0)
    ci = lax.broadcasted_iota(jnp.int32, (CH, CH), 1)
    pos = ri.astype(F32)
    diff = (ri - ci).astype(F32)
    scale = RET_K ** -0.5
    rd = rd_ref[...]
    lg_all = jnp.minimum(rd, 0.0) - jnp.log(1.0 + jnp.exp(-jnp.abs(rd)))

    for d in range(2):
        lg = lg_all[d:d + 1, :]
        if d == 0:
            dm = jnp.exp(jnp.where(ri >= ci, diff * lg, -jnp.inf))
            kte = jnp.exp((CH - 1.0 - pos) * lg)
            qfs = jnp.exp((pos + 1.0) * lg)
        else:
            dm = jnp.exp(jnp.where(ci > ri, -diff * lg, -jnp.inf))
            kte = jnp.exp(pos * lg)
            qfs = jnp.exp((CH - pos) * lg)
        cdec = jnp.exp(CH * lg)
        cdec2 = jnp.concatenate([cdec, cdec], axis=1)
        if latent:
            st_ref[...] = (s0f_ref if d == 0 else s0b_ref)[...]
        else:
            st_ref[...] = jnp.zeros_like(st_ref)

        def chunk(c, carry, d=d, dm=dm, kte=kte, qfs=qfs, cdec2=cdec2):
            ch = c if d == 0 else nc - 1 - c
            t0 = pl.multiple_of(ch * CH, CH)
            q = q_ref[pl.ds(t0, CH), :] * scale
            k = k_ref[pl.ds(t0, CH), :]
            if latent:
                cos = cos_ref[pl.ds(t0, CH), :]
                sa = sa_ref[pl.ds(t0, CH), :]
                sb = sb_ref[pl.ds(t0, CH), :]
                q = _rope(q, cos, sa, sb)
                k = _rope(k, cos, sa, sb)
            vb = v_ref[pl.ds(t0, CH), :].astype(BF16)
            sc = _dot_nt(q.astype(BF16), k.astype(BF16)) * dm
            st = st_ref[...]
            o = _dot(sc.astype(BF16), vb) + _dot((q * qfs).astype(BF16), st.astype(BF16))
            st_ref[...] = cdec2 * st + _dot((k * kte).T.astype(BF16), vb)
            if d == 0:
                of_ref[pl.ds(t0, CH), :] = o
            else:
                tot = of_ref[pl.ds(t0, CH), :] + o
                y = _silu(rg_ref[pl.ds(t0, CH), :]) * _rms(tot, ng_ref[...])
                o_ref[pl.ds(t0, CH), :] = y.astype(BF16)
            return carry

        lax.fori_loop(0, nc, chunk, 0, unroll=min(nc, 4))
        if out_state:
            sout_refs[d][...] = st_ref[...]


def _ret(proj, row_off, nb, seq, rd, ng, layer, rope=None, s0=None, out_state=False, dst=None):
    latent = s0 is not None
    rb = row_off // seq
    in_specs = [
        pl.BlockSpec((seq, RET_K), lambda b, h: (rb + b, OFF_RQ // RET_K + h)),
        pl.BlockSpec((seq, RET_K), lambda b, h: (rb + b, OFF_RK // RET_K + h)),
        pl.BlockSpec((seq, RET_V), lambda b, h: (rb + b, OFF_RV // RET_V + h)),
        pl.BlockSpec((seq, RET_V), lambda b, h: (rb + b, OFF_RG // RET_V + h)),
        pl.BlockSpec((None, None, 2, 128), lambda b, h: (layer, h, 0, 0)),
        pl.BlockSpec((None, 1, RET_V), lambda b, h: (layer, 0, h)),
    ]
    args = [proj, proj, proj, proj, rd, ng]
    if latent:
        ts = pl.BlockSpec((seq, RET_K), lambda b, h: (0, 0))
        ss = pl.BlockSpec((None, None, None, RET_K, RET_V), lambda b, h: (b, layer, h, 0, 0))
        in_specs += [ts, ts, ts, ss, ss]
        args += [rope[0], rope[1], rope[2], s0[0], s0[1]]
    aliases = {}
    if dst is not None:
        in_specs.append(pl.BlockSpec(memory_space=pl.ANY))
        args.append(dst)
        aliases = {len(args) - 1: 0}
    out_specs = [pl.BlockSpec((seq, RET_V), lambda b, h: (rb + b, h))]
    out_shape = [jax.ShapeDtypeStruct((proj.shape[0], D), BF16)]
    if out_state:
        so = pl.BlockSpec((None, None, RET_K, RET_V), lambda b, h: (b, h, 0, 0))
        out_specs += [so, so]
        out_shape += [jax.ShapeDtypeStruct((nb, RET_H, RET_K, RET_V), F32)] * 2
    return pl.pallas_call(
        functools.partial(_ret_kernel, seq=seq, latent=latent, out_state=out_state, has_dst=dst is not None),
        input_output_aliases=aliases,
        grid=(nb, RET_H),
        in_specs=in_specs,
        out_specs=out_specs,
        out_shape=out_shape,
        scratch_shapes=[pltpu.VMEM((seq, RET_V), F32), pltpu.VMEM((RET_K, RET_V), F32)],
        compiler_params=_cp(("parallel", "parallel")),
        name="ret_latent" if latent else "ret_ctx",
    )(*args)


def _merge_kernel(ys_ref, z_ref, ng_ref, ya_ref, yr_ref, g0_ref, g1_ref, g2_ref, wb_ref, o_ref, y0_ref):
    @pl.when(pl.program_id(1) == 0)
    def _():
        y = ys_ref[...] * _silu(z_ref[...])
        y0_ref[...] = _rms(y, ng_ref[...]).astype(BF16)

    acc = _sigmoid(g0_ref[...]) * _dot(y0_ref[...], wb_ref[0])
    acc += _sigmoid(g1_ref[...]) * _dot(ya_ref[...], wb_ref[1])
    acc += _sigmoid(g2_ref[...]) * _dot(yr_ref[...], wb_ref[2])
    o_ref[...] = acc.astype(BF16)


def _merge(y_ssd, proj, ssd_ng, y_att, y_ret, w_branch, layer, tm=512, tn=512):
    t = y_ssd.shape[0]
    gl = OFF_GL // tn
    nj = D // tn
    return pl.pallas_call(
        _merge_kernel,
        grid=(t // tm, nj),
        in_specs=[pl.BlockSpec((tm, D), lambda i, j: (i, 0)),
                  pl.BlockSpec((tm, D), lambda i, j: (i, OFF_Z // D)),
                  pl.BlockSpec((None, 1, D), lambda i, j: (layer, 0, 0)),
                  pl.BlockSpec((tm, D), lambda i, j: (i, 0)),
                  pl.BlockSpec((tm, D), lambda i, j: (i, 0)),
                  pl.BlockSpec((tm, tn), lambda i, j: (i, gl + j)),
                  pl.BlockSpec((tm, tn), lambda i, j: (i, gl + nj + j)),
                  pl.BlockSpec((tm, tn), lambda i, j: (i, gl + 2 * nj + j)),
                  pl.BlockSpec((None, 3, D, tn), lambda i, j: (layer, 0, 0, j))],
        out_specs=pl.BlockSpec((tm, tn), lambda i, j: (i, j)),
        out_shape=jax.ShapeDtypeStruct((t, D), BF16),
        scratch_shapes=[pltpu.VMEM((tm, D), BF16)],
        compiler_params=_cp(("parallel", "arbitrary")),
        name="merge",
    )(y_ssd, proj, ssd_ng, y_att, y_ret, proj, proj, proj, w_branch)


def _outproj_kernel(m_ref, w_ref, x_ref, g1_ref, ng_ref, sh_ref, sc_ref, wr_ref,
                    x1_ref, h2_ref, aff_ref, afft_ref):
    x1 = x_ref[...] + g1_ref[...] * _dot(m_ref[...], w_ref[...])
    x1_ref[...] = x1
    h = _rms(x1, ng_ref[...]) * (1.0 + sc_ref[...]) + sh_ref[...]
    hb = h.astype(BF16)
    h2_ref[...] = hb
    hl = (h - hb.astype(F32)).astype(BF16)
    logits = _dot(hb, wr_ref[0]) + _dot(hl, wr_ref[0]) + _dot(hb, wr_ref[1])
    lane = lax.broadcasted_iota(jnp.int32, logits.shape, 1)
    logits = jnp.where(lane < N_EXP, logits, -jnp.inf)
    e = jnp.exp(logits - jnp.max(logits, axis=-1, keepdims=True))
    aff = e / jnp.sum(e, axis=-1, keepdims=True)
    aff_ref[...] = aff
    afft_ref[...] = aff.T[0:N_EXP, :]


def _outproj(merged, w_out, x, mods, norm_g, w_router, layer, mod_row, tm=512):
    t = x.shape[0]

    def ms(col):
        return pl.BlockSpec((None, None, 1, D), lambda i: (layer, mod_row(i, tm), 0, col))

    return pl.pallas_call(
        _outproj_kernel,
        grid=(t // tm,),
        in_specs=[pl.BlockSpec((tm, D), lambda i: (i, 0)),
                  pl.BlockSpec((None, D, D), lambda i: (layer, 0, 0)),
                  pl.BlockSpec((tm, D), lambda i: (i, 0)),
                  ms(2),
                  pl.BlockSpec((None, 1, D), lambda i: (layer, 0, 0)),
                  ms(3), ms(4),
                  pl.BlockSpec((None, 2, D, 128), lambda i: (layer, 0, 0, 0))],
        out_specs=[pl.BlockSpec((tm, D), lambda i: (i, 0)),
                   pl.BlockSpec((tm, D), lambda i: (i, 0)),
                   pl.BlockSpec((tm, 128), lambda i: (i, 0)),
                   pl.BlockSpec((N_EXP, tm), lambda i: (0, i))],
        out_shape=[jax.ShapeDtypeStruct((t, D), F32),
                   jax.ShapeDtypeStruct((t, D), BF16),
                   jax.ShapeDtypeStruct((t, 128), F32),
                   jax.ShapeDtypeStruct((N_EXP, t), F32)],
        compiler_params=_cp(("parallel",)),
        name="out_proj",
    )(merged, w_out, x, mods, norm_g, mods, mods, w_router)


SUB = 128
WIN = SUB + 16


def _select_kernel(a_ref, post_ref, postok_ref, st_ref, cnt_ref, *, groups):
    ri = lax.broadcasted_iota(jnp.int32, (SUB, SUB), 0)
    ci = lax.broadcasted_iota(jnp.int32, (SUB, SUB), 1)
    before = jnp.where(ri < ci, 1.0, 0.0).astype(BF16)
    lane = lax.broadcasted_iota(jnp.int32, (N_EXP, SUB), 1)
    st_ref[...] = jnp.zeros_like(st_ref)
    for lo, n, cap, base in groups:
        bits = pltpu.bitcast(a_ref[:, lo:lo + n], jnp.int32)

        def search(i, prefix, bits=bits, cap=cap):
            cand = prefix | jnp.left_shift(jnp.int32(1), 30 - i)
            cnt = jnp.sum(jnp.where(bits >= cand, 1.0, 0.0), axis=1, keepdims=True)
            return jnp.where(cnt >= cap, cand, prefix)

        zero = pltpu.bitcast(jnp.minimum(jnp.min(a_ref[:, lo:lo + n], axis=1, keepdims=True), 0.0), jnp.int32)
        tau = lax.fori_loop(0, 31, search, zero)
        need = cap - jnp.sum(jnp.where(bits > tau, 1.0, 0.0), axis=1, keepdims=True)

        cnt_ref[...] = jnp.zeros_like(cnt_ref)

        def block(b, carry, lo=lo, base=base, tau=tau, need=need):
            ceq = cnt_ref[0][:, 0:1]
            csel = cnt_ref[1][:, 0:1]
            t0 = pl.multiple_of(lo + b * SUB, SUB)
            bb = pltpu.bitcast(a_ref[:, pl.ds(t0, SUB)], jnp.int32)
            eq = jnp.where(bb == tau, 1.0, 0.0)
            rank_eq = ceq + _dot(eq.astype(BF16), before)
            sel = jnp.where((bb > tau) | ((bb == tau) & (rank_eq < need)), 1.0, 0.0)
            pos = csel + _dot(sel.astype(BF16), before) + base
            pm = jnp.where(sel > 0.0, pos, -1.0)
            post_ref[:, pl.ds(t0, SUB)] = pm
            full = jnp.concatenate([pm, jnp.full((SUB - N_EXP, SUB), -1.0, F32)], axis=0)
            postok_ref[pl.ds(t0, SUB), :] = full.T
            st_ref[...] = jnp.where(lane == lo // SUB + b, (csel + base).astype(jnp.int32), st_ref[...])
            cnt_ref[0] = jnp.broadcast_to(ceq + jnp.sum(eq, axis=1, keepdims=True), (N_EXP, SUB))
            cnt_ref[1] = jnp.broadcast_to(csel + jnp.sum(sel, axis=1, keepdims=True), (N_EXP, SUB))
            return carry

        lax.fori_loop(0, n // SUB, block, 0)


def _select(afft, groups):
    t = afft.shape[1]
    assert t // SUB <= SUB
    return pl.pallas_call(
        functools.partial(_select_kernel, groups=groups),
        out_shape=[jax.ShapeDtypeStruct((N_EXP, t), F32),
                   jax.ShapeDtypeStruct((t, SUB), F32),
                   jax.ShapeDtypeStruct((N_EXP, SUB), jnp.int32)],
        scratch_shapes=[pltpu.VMEM((2, N_EXP, SUB), F32)],
        compiler_params=pltpu.CompilerParams(vmem_limit_bytes=VMEM_LIMIT),
        name="select",
    )(afft)


def _ffn_kernel(st_ref, h_ref, pos_ref, w1_ref, w3_ref, w2_ref, o_ref, stage_ref, hid_ref,
                *, n_e, nbd, nf, n2, cap, tbd, tf):
    r = pl.program_id(0)
    s = pl.program_id(1)

    @pl.when((r < n_e) & (s < nbd))
    def _():
        slot = r % 2

        @pl.when(s == 0)
        def _():
            stage_ref[slot] = jnp.zeros(stage_ref.shape[1:], BF16)

        srow = lax.broadcasted_iota(jnp.int32, (WIN, SUB), 0).astype(F32)
        for k in range(tbd // SUB):
            off16 = pl.multiple_of((st_ref[r, s * (tbd // SUB) + k] // 16) * 16, 16)
            rel = pos_ref[:, k * SUB:(k + 1) * SUB] - off16.astype(F32)
            onehot = jnp.where(srow == rel, 1.0, 0.0).astype(BF16)
            rows = _dot(onehot, h_ref[k * SUB:(k + 1) * SUB, :]).astype(BF16)
            stage_ref[slot, pl.ds(off16, WIN), :] += rows

    @pl.when((r > 0) & (s < nf))
    def _():
        x = stage_ref[(r + 1) % 2, 0:cap, :]
        hid = _silu(_dot(x, w1_ref[...].astype(BF16))) * _dot(x, w3_ref[...].astype(BF16))
        hid_ref[:, pl.ds(pl.multiple_of(s * tf, tf), tf)] = hid.astype(BF16)

    @pl.when((r > 0) & (s >= nf) & (s < nf + n2))
    def _():
        o_ref[...] = _dot(hid_ref[...], w2_ref[...].astype(BF16)).astype(BF16)


def _ffn(starts, h2, post, w1, w3, w2, layer, cap, tf=256, tn=256):
    t = h2.shape[0]
    n_e, ff = w1.shape[1], w1.shape[-1]
    nf = ff // tf
    n2 = D // tn
    tbd = max(SUB, (t // (nf + n2)) // SUB * SUB)
    while t % tbd:
        tbd -= SUB
    nbd = t // tbd
    steps = max(nbd, nf + n2)

    def tok(r, s):
        return jnp.where(r < n_e, jnp.minimum(s, nbd - 1), nbd - 1)

    def prev(r):
        return jnp.maximum(r - 1, 0)

    def fcol(s):
        return jnp.minimum(s, nf - 1)

    def ocol(r, s):
        return jnp.where(r > 0, jnp.clip(s - nf, 0, n2 - 1), 0)

    grid_spec = pltpu.PrefetchScalarGridSpec(
        num_scalar_prefetch=1,
        grid=(n_e + 1, steps),
        in_specs=[pl.BlockSpec((tbd, D), lambda r, s, st: (tok(r, s), 0)),
                  pl.BlockSpec((None, 1, tbd), lambda r, s, st: (jnp.minimum(r, n_e - 1), 0, tok(r, s))),
                  pl.BlockSpec((None, None, D, tf), lambda r, s, st: (layer, prev(r), 0, fcol(s))),
                  pl.BlockSpec((None, None, D, tf), lambda r, s, st: (layer, prev(r), 0, fcol(s))),
                  pl.BlockSpec((None, None, ff, tn), lambda r, s, st: (layer, prev(r), 0, ocol(r, s)))],
        out_specs=pl.BlockSpec((None, cap, tn), lambda r, s, st: (prev(r), 0, ocol(r, s))),
        scratch_shapes=[pltpu.VMEM((2, cap + WIN, D), BF16), pltpu.VMEM((cap, ff), BF16)],
    )
    return pl.pallas_call(
        functools.partial(_ffn_kernel, n_e=n_e, nbd=nbd, nf=nf, n2=n2, cap=cap, tbd=tbd, tf=tf),
        grid_spec=grid_spec,
        out_shape=jax.ShapeDtypeStruct((n_e, cap, D), BF16),
        compiler_params=_cp(("arbitrary", "arbitrary")),
        name="expert_ffn",
    )(starts, h2, post.reshape(n_e, 1, t), w1, w3, w2)


def _combine_kernel(st_ref, ye_hbm, postok_ref, aff_ref, x1_ref, g2_ref, o_ref, win_ref, sem, *, cap):
    b = pl.program_id(0)
    nb = pl.num_programs(0)

    def first_slot(bb, e):
        return pl.multiple_of(jnp.minimum((st_ref[e, bb] // 16) * 16, cap - WIN), 16)

    def window_copy(bb, slot, e):
        return pltpu.make_async_copy(ye_hbm.at[e, pl.ds(first_slot(bb, e), WIN), :],
                                     win_ref.at[slot, pl.ds(e * WIN, WIN), :], sem.at[slot])

    @pl.when(b == 0)
    def _():
        tail = jnp.zeros((2 * SUB - WIN, D), BF16)
        win_ref[0, N_EXP * WIN:, :] = tail
        win_ref[1, N_EXP * WIN:, :] = tail
        for e in range(N_EXP):
            window_copy(0, 0, e).start()

    @pl.when(b + 1 < nb)
    def _():
        for e in range(N_EXP):
            window_copy(b + 1, (b + 1) % 2, e).start()

    slot = b % 2
    for e in range(N_EXP):
        window_copy(b, slot, e).wait()

    lane = lax.broadcasted_iota(jnp.int32, (SUB, 2 * SUB), 1).astype(F32)
    acc = jnp.zeros((SUB, D), F32)
    for e in range(N_EXP):
        rel = postok_ref[:, e:e + 1] - first_slot(b, e).astype(F32)
        gated = jnp.where(lane == rel, aff_ref[:, e:e + 1], 0.0).astype(BF16)
        acc += _dot(gated, win_ref[slot, pl.ds(e * WIN, 2 * SUB), :])
    o_ref[...] = x1_ref[...] + g2_ref[...] * acc


def _combine(starts, ye, postok, aff, x1, mods, layer, mod_row):
    t = x1.shape[0]
    n_e, cap, _ = ye.shape
    grid_spec = pltpu.PrefetchScalarGridSpec(
        num_scalar_prefetch=1,
        grid=(t // SUB,),
        in_specs=[pl.BlockSpec(memory_space=pl.ANY),
                  pl.BlockSpec((SUB, SUB), lambda b, st: (b, 0)),
                  pl.BlockSpec((SUB, SUB), lambda b, st: (b, 0)),
                  pl.BlockSpec((SUB, D), lambda b, st: (b, 0)),
                  pl.BlockSpec((None, None, 1, D), lambda b, st: (layer, mod_row(b, SUB), 0, 5))],
        out_specs=pl.BlockSpec((SUB, D), lambda b, st: (b, 0)),
        scratch_shapes=[pltpu.VMEM((2, N_EXP * WIN + 2 * SUB - WIN, D), BF16),
                        pltpu.SemaphoreType.DMA((2,))],
    )
    return pl.pallas_call(
        functools.partial(_combine_kernel, cap=cap),
        grid_spec=grid_spec,
        out_shape=jax.ShapeDtypeStruct((t, D), F32),
        compiler_params=_cp(("arbitrary",)),
        name="combine",
    )(starts, ye, postok, aff, x1, mods)


def _rope_tables(n_tokens):
    rows_n = n_tokens // GRID_W
    row = jnp.repeat(jnp.arange(rows_n, dtype=F32), GRID_W)
    col = jnp.tile(jnp.arange(GRID_W, dtype=F32), rows_n)
    n_freq = ATT_D // 4
    inv = ROPE_THETA ** (-jnp.arange(n_freq, dtype=F32) / n_freq)
    ang_r = row[:, None] * inv
    ang_c = col[:, None] * inv
    zeros = jnp.zeros_like(ang_r)
    cos = jnp.concatenate([jnp.cos(ang_r)] * 2 + [jnp.cos(ang_c)] * 2, axis=1)
    sa = jnp.concatenate([-jnp.sin(ang_r), zeros, -jnp.sin(ang_c), zeros], axis=1)
    sb = jnp.concatenate([zeros, jnp.sin(ang_r), zeros, jnp.sin(ang_c)], axis=1)
    return cos, sa, sb


def _group_dt_lanes(v):
    depth = v.shape[0]
    v = v.reshape(depth, 2, SSD_G, SSD_R).transpose(0, 2, 1, 3).reshape(depth, SSD_G, 1, 2 * SSD_R)
    return jnp.pad(v, ((0, 0), (0, 0), (0, 0), (0, 128 - 2 * SSD_R)))


def _prep_w_in(w_in):
    depth = w_in.shape[0]
    dt = w_in[:, :, ORIG_DT:ORIG_AFTER_DT].reshape(depth, D, 2, SSD_G, SSD_R)
    dt = dt.transpose(0, 1, 3, 2, 4).reshape(depth, D, SSD_G, 2 * SSD_R)
    dt = jnp.pad(dt, ((0, 0), (0, 0), (0, 0), (0, 128 - 2 * SSD_R))).reshape(depth, D, SSD_G * 128)
    w = jnp.concatenate([w_in[:, :, :ORIG_DT], w_in[:, :, ORIG_AFTER_DT:], dt], axis=2)
    return w.astype(BF16)


def kernel(x_prompt, x_sample, cache_attn_k, cache_attn_v, state_ssd_fwd, state_ssd_bwd, state_ret_fwd, state_ret_bwd, c, c_ctx, norm1_g, norm2_g, w_ada, b_ada, w_in, ssd_conv_w, ssd_conv_b, ssd_a_log, ssd_dt_bias, ssd_d, ssd_norm_g, q_norm_g, k_norm_g, ret_decay, ret_norm_g, w_branch, w_out, w_router, w_e1, w_e3, w_e2):
    depth = w_in.shape[0]
    pb, pseq, _ = x_prompt.shape
    sb, sseq, _ = x_sample.shape
    n_p = pb * pseq
    n_s = sb * sseq
    assert n_p % 1024 == 0 and sseq % 1024 == 0 and 1 + sb <= 8

    def mod_row(i, tm):
        return jnp.where(i < n_p // tm, 0, 1 + (i * tm - n_p) // sseq)

    w_in_b = _prep_w_in(w_in)
    w_branch_b = w_branch.astype(BF16)
    w_out_b = w_out.astype(BF16)
    cap_p = CAP_FACTOR * n_p // N_EXP
    cap_s = CAP_FACTOR * n_s // N_EXP
    groups = ((0, n_p, cap_p, 0), (n_p, n_s, cap_s, cap_p))
    w_router_f = jnp.pad(w_router, ((0, 0), (0, 0), (0, 128 - N_EXP)))
    w_router_hi = w_router_f.astype(BF16)
    w_router_p = jnp.stack([w_router_hi, (w_router_f - w_router_hi.astype(F32)).astype(BF16)], axis=1)
    dtb = _group_dt_lanes(ssd_dt_bias)
    alog = _group_dt_lanes(ssd_a_log)
    dvec = jnp.repeat(ssd_d, SSD_P, axis=1).reshape(depth, 1, D)
    rd = jnp.broadcast_to(ret_decay.transpose(0, 2, 1)[..., None], (depth, RET_H, 2, 128))
    rope = _rope_tables(sseq)
    n1g = norm1_g.reshape(depth, 1, D)
    n2g = norm2_g.reshape(depth, 1, D)
    ssd_ng = ssd_norm_g.reshape(depth, 1, D)
    ret_ng = ret_norm_g.reshape(depth, 1, D)
    qg = q_norm_g.reshape(depth, 1, ATT_D)
    kg = k_norm_g.reshape(depth, 1, ATT_D)

    cond8 = jnp.zeros((8, D), F32).at[0].set(c_ctx).at[1:1 + sb].set(c)
    mods = _ada(cond8, w_ada, b_ada).reshape(depth, 8, 1, 6 * D)

    x = jnp.concatenate([x_prompt.reshape(n_p, D), x_sample.reshape(n_s, D)], axis=0)
    new_k, new_v, new_hf, new_hb, new_sf, new_sb = [], [], [], [], [], []
    for l in range(depth):
        proj = _inproj(x, n1g, mods, w_in_b, l, mod_row)

        ys_p, hf, hb = _ssd(proj, 0, pb, pseq, ssd_conv_w, ssd_conv_b.reshape(depth, 1, -1), dtb, alog, dvec,
                            l, out_state=True)
        (y_ssd,) = _ssd(proj, n_p, sb, sseq, ssd_conv_w, ssd_conv_b.reshape(depth, 1, -1), dtb, alog, dvec,
                        l, h0=(state_ssd_fwd, state_ssd_bwd), dst=ys_p)
        ya_p, kn, vn = _att(proj, 0, pb, pseq, qg, kg, l, tq=pseq)
        (y_att,) = _att(proj, n_p, sb, sseq, qg, kg, l, tq=256, ctx=(cache_attn_k, cache_attn_v), rope=rope,
                        dst=ya_p)
        yr_p, sf, sbk = _ret(proj, 0, pb, pseq, rd, ret_ng, l, out_state=True)
        (y_ret,) = _ret(proj, n_p, sb, sseq, rd, ret_ng, l, rope=rope, s0=(state_ret_fwd, state_ret_bwd),
                        dst=yr_p)
        new_k.append(kn), new_v.append(vn), new_hf.append(hf), new_hb.append(hb)
        new_sf.append(sf), new_sb.append(sbk)

        merged = _merge(y_ssd, proj, ssd_ng, y_att, y_ret, w_branch_b, l)
        x1, h2, aff, afft = _outproj(merged, w_out_b, x, mods, n2g, w_router_p, l, mod_row)
        post, postok, starts = _select(afft, groups)
        ye = _ffn(starts, h2, post, w_e1, w_e3, w_e2, l, cap_p + cap_s)
        x = _combine(starts, ye, postok, aff, x1, mods, l, mod_row)

    y_prompt = x[:n_p].reshape(pb, pseq, D)
    y_sample = x[n_p:].reshape(sb, sseq, D)
    return (y_prompt, y_sample, jnp.stack(new_k, axis=1), jnp.stack(new_v, axis=1),
            jnp.stack(new_hf, axis=1), jnp.stack(new_hb, axis=1),
            jnp.stack(new_sf, axis=1), jnp.stack(new_sb, axis=1))
```

```python
import functools

import jax
import jax.numpy as jnp
from jax import lax
from jax.experimental import pallas as pl
from jax.experimental.pallas import tpu as pltpu

F32 = jnp.float32
BF16 = jnp.bfloat16
EPS = 1e-6
ROPE_THETA = 10000.0
GRID_W = 64

D = 2048
CH = 128
SSD_P = 64
SSD_N = 128
SSD_G = 4
SSD_H = 32
SSD_R = SSD_H // SSD_G
ATT_D = 128
ATT_H = 16
ATT_G = 4
ATT_R = ATT_H // ATT_G
RET_H = 8
RET_K = 128
RET_V = 256
N_EXP = 16
CAP_FACTOR = 2

OFF_Z = 0
OFF_X = 2048
OFF_B = 4096
OFF_C = 4608
OFF_AQ = 5120
OFF_AK = 7168
OFF_AV = 7680
OFF_RQ = 8192
OFF_RK = 9216
OFF_RV = 10240
OFF_RG = 12288
OFF_GL = 14336
OFF_DT = 20480
NW = OFF_DT + SSD_G * 128
ORIG_DT = 5120
ORIG_AFTER_DT = 5184
ORIG_W = 20544

VMEM_LIMIT = 56 * 1024 * 1024


def _cp(sem):
    return pltpu.CompilerParams(dimension_semantics=sem, vmem_limit_bytes=VMEM_LIMIT)


def _sigmoid(x):
    return 1.0 / (1.0 + jnp.exp(-x))


def _silu(x):
    return x * _sigmoid(x)


def _softplus(x):
    return jnp.maximum(x, 0.0) + jnp.log(1.0 + jnp.exp(-jnp.abs(x)))


def _rms(x, g):
    ms = jnp.mean(x * x, axis=-1, keepdims=True)
    return x * lax.rsqrt(ms + EPS) * g


def _dot(a, b):
    return jnp.dot(a, b, preferred_element_type=F32)


def _dot_nt(a, b):
    return lax.dot_general(a, b, (((1,), (1,)), ((), ())), preferred_element_type=F32)


def _rope(x, cos, sa, sb):
    return x * cos + pltpu.roll(x, 96, 1) * sa + pltpu.roll(x, 32, 1) * sb


def _ada_kernel(c_ref, w_ref, b_ref, o_ref):
    s = _silu(c_ref[...]).astype(BF16)
    o_ref[...] = _dot(s, w_ref[...].astype(BF16)) + b_ref[...]


def _ada(cond8, w_ada, b_ada):
    depth, _, n6 = w_ada.shape
    tn = 1024
    return pl.pallas_call(
        _ada_kernel,
        grid=(depth, n6 // tn),
        in_specs=[pl.BlockSpec((8, D), lambda l, j: (0, 0)),
                  pl.BlockSpec((None, D, tn), lambda l, j: (l, 0, j)),
                  pl.BlockSpec((None, 1, tn), lambda l, j: (l, 0, j))],
        out_specs=pl.BlockSpec((None, 8, tn), lambda l, j: (l, 0, j)),
        out_shape=jax.ShapeDtypeStruct((depth, 8, n6), F32),
        compiler_params=_cp(("parallel", "parallel")),
        name="ada_mod",
    )(cond8, w_ada, b_ada.reshape(depth, 1, n6))


def _inproj_kernel(x_ref, g_ref, sh_ref, sc_ref, w_ref, o_ref, hn_ref):
    @pl.when(pl.program_id(1) == 0)
    def _():
        h = _rms(x_ref[...], g_ref[...])
        hn_ref[...] = (h * (1.0 + sc_ref[...]) + sh_ref[...]).astype(BF16)

    o_ref[...] = _dot(hn_ref[...], w_ref[...])


def _inproj(x, norm_g, mods, w_in, layer, mod_row, tm=1024, tn=512):
    t = x.shape[0]
    return pl.pallas_call(
        _inproj_kernel,
        grid=(t // tm, NW // tn),
        in_specs=[pl.BlockSpec((tm, D), lambda i, j: (i, 0)),
                  pl.BlockSpec((None, 1, D), lambda i, j: (layer, 0, 0)),
                  pl.BlockSpec((None, None, 1, D), lambda i, j: (layer, mod_row(i, tm), 0, 0)),
                  pl.BlockSpec((None, None, 1, D), lambda i, j: (layer, mod_row(i, tm), 0, 1)),
                  pl.BlockSpec((None, D, tn), lambda i, j: (layer, 0, j))],
        out_specs=pl.BlockSpec((tm, tn), lambda i, j: (i, j)),
        out_shape=jax.ShapeDtypeStruct((t, NW), F32),
        scratch_shapes=[pltpu.VMEM((tm, D), BF16)],
        compiler_params=_cp(("parallel", "arbitrary")),
        name="in_proj",
    )(x, norm_g, mods, mods, w_in)


def _ssd_kernel(*refs, seq, has_h0, out_state, has_dst):
    it = iter(refs)
    x_ref, b_ref, c_ref, dt_ref = next(it), next(it), next(it), next(it)
    cwx_ref, cwb_ref, cwc_ref = next(it), next(it), next(it)
    cbx_ref, cbb_ref, cbc_ref = next(it), next(it), next(it)
    dtb_ref, alog_ref, dvec_ref = next(it), next(it), next(it)
    h0_refs = (next(it), next(it)) if has_h0 else None
    if has_dst:
        next(it)
    y_ref = next(it)
    hout_refs = (next(it), next(it)) if out_state else None
    xs_ref, bs_ref, cs_ref, st_ref = next(it), next(it), next(it), next(it)

    nc = seq // CH
    row1 = lax.broadcasted_iota(jnp.int32, (CH, 1), 0)

    def conv_chunk(c, carry):
        t0 = pl.multiple_of(c * CH, CH)
        pidx = pl.multiple_of(jnp.maximum(t0 - 8, 0), 8)
        nidx = pl.multiple_of(jnp.minimum(t0 + CH, seq - 8), 8)

        def conv(src, w_ref, bias_ref, dst):
            cur = src[pl.ds(t0, CH), :]
            prev_row = jnp.where(c > 0, src[pl.ds(pidx, 8), :][7:8, :], 0.0)
            next_row = jnp.where(c < nc - 1, src[pl.ds(nidx, 8), :][0:1, :], 0.0)
            xm = jnp.where(row1 == 0, prev_row, pltpu.roll(cur, 1, 0))
            xp = jnp.where(row1 == CH - 1, next_row, pltpu.roll(cur, CH - 1, 0))
            w = w_ref[...]
            v = xm * w[0:1, :] + cur * w[1:2, :] + xp * w[2:3, :] + bias_ref[...]
            v = _silu(v)
            dst[pl.ds(t0, CH), :] = v
            return v

        xs = conv(x_ref, cwx_ref, cbx_ref, xs_ref)
        y_ref[pl.ds(t0, CH), :] = dvec_ref[...] * xs
        conv(b_ref, cwb_ref, cbb_ref, bs_ref)
        conv(c_ref, cwc_ref, cbc_ref, cs_ref)
        return carry

    lax.fori_loop(0, nc, conv_chunk, 0)

    ri = lax.broadcasted_iota(jnp.int32, (CH, CH), 0)
    ci = lax.broadcasted_iota(jnp.int32, (CH, CH), 1)
    tri = (ri >= ci).astype(F32)
    left = ci < SSD_P
    left_row = ci[0:1, :] < SSD_P
    dtb = dtb_ref[...]
    aneg = -jnp.exp(alog_ref[...])

    for d in range(2):
        for pair in range(SSD_R // 2):
            if has_h0:
                st_ref[d, pair] = h0_refs[d][2 * pair:2 * pair + 2].reshape(2 * SSD_P, SSD_N).T
            else:
                st_ref[d, pair] = jnp.zeros((SSD_N, 2 * SSD_P), F32)

    def scan_one(d, ch):
        mask = (ri >= ci) if d == 0 else (ci >= ri)
        t0 = pl.multiple_of(ch * CH, CH)
        dtv = _softplus(dt_ref[pl.ds(t0, CH), :] + dtb)
        a = dtv * aneg
        pref = jnp.dot(tri, a, precision=lax.Precision.HIGHEST, preferred_element_type=F32)
        tot = pref[CH - 1:CH, :]
        s = pref if d == 0 else tot - pref + a
        cd = jnp.exp(tot)
        s_t = s.T
        dt_t = dtv.T
        bc = bs_ref[pl.ds(t0, CH), :]
        bc_t = bc.T
        bcb = bc.astype(BF16)
        ccb = cs_ref[pl.ds(t0, CH), :].astype(BF16)
        cb = _dot_nt(ccb, bcb)
        for pair in range(SSD_R // 2):
            la = d * SSD_R + 2 * pair
            lb = la + 1
            x2 = xs_ref[pl.ds(t0, CH), pair * 128:(pair + 1) * 128]
            x_a = jnp.where(left, x2, 0.0).astype(BF16)
            x_b = jnp.where(left, 0.0, x2).astype(BF16)
            sa = jnp.broadcast_to(s[:, la:la + 1], (CH, CH))
            sb = jnp.broadcast_to(s[:, lb:lb + 1], (CH, CH))
            sa_t, sb_t = s_t[la:la + 1, :], s_t[lb:lb + 1, :]
            dta_t, dtb_t = dt_t[la:la + 1, :], dt_t[lb:lb + 1, :]
            w_a = (cb * jnp.exp(jnp.where(mask, sa - sa_t, -jnp.inf)) * dta_t).astype(BF16)
            w_b = (cb * jnp.exp(jnp.where(mask, sb - sb_t, -jnp.inf)) * dtb_t).astype(BF16)
            st = st_ref[d, pair]
            y = _dot(w_a, x_a) + _dot(w_b, x_b)
            y = y + _dot(ccb, st.astype(BF16)) * jnp.exp(jnp.where(left, sa, sb))
            g_a = jnp.exp(tot[:, la:la + 1] - sa_t) * dta_t
            g_b = jnp.exp(tot[:, lb:lb + 1] - sb_t) * dtb_t
            new = _dot((bc_t * g_a).astype(BF16), x_a) + _dot((bc_t * g_b).astype(BF16), x_b)
            cd2 = jnp.where(left_row, cd[:, la:la + 1], cd[:, lb:lb + 1])
            st_ref[d, pair] = cd2 * st + new
            y_ref[pl.ds(t0, CH), pair * 128:(pair + 1) * 128] += y

    def scan_chunk(c, carry):
        scan_one(0, c)
        scan_one(1, nc - 1 - c)
        return carry

    lax.fori_loop(0, nc, scan_chunk, 0)
    if out_state:
        for d in range(2):
            for pair in range(SSD_R // 2):
                hout_refs[d][2 * pair:2 * pair + 2] = st_ref[d, pair].T.reshape(2, SSD_P, SSD_N)


def _ssd(proj, row_off, nb, seq, conv_w, conv_b, dtb, alog, dvec, layer, h0=None, out_state=False, dst=None):
    rb = row_off // seq
    has_h0 = h0 is not None
    in_specs = [
        pl.BlockSpec((seq, 512), lambda b, g: (rb + b, OFF_X // 512 + g)),
        pl.BlockSpec((seq, 128), lambda b, g: (rb + b, OFF_B // 128 + g)),
        pl.BlockSpec((seq, 128), lambda b, g: (rb + b, OFF_C // 128 + g)),
        pl.BlockSpec((seq, 128), lambda b, g: (rb + b, OFF_DT // 128 + g)),
        pl.BlockSpec((None, 3, 512), lambda b, g: (layer, 0, g)),
        pl.BlockSpec((None, 3, 128), lambda b, g: (layer, 0, D // 128 + g)),
        pl.BlockSpec((None, 3, 128), lambda b, g: (layer, 0, D // 128 + SSD_G + g)),
        pl.BlockSpec((None, 1, 512), lambda b, g: (layer, 0, g)),
        pl.BlockSpec((None, 1, 128), lambda b, g: (layer, 0, D // 128 + g)),
        pl.BlockSpec((None, 1, 128), lambda b, g: (layer, 0, D // 128 + SSD_G + g)),
        pl.BlockSpec((None, None, 1, 128), lambda b, g: (layer, g, 0, 0)),
        pl.BlockSpec((None, None, 1, 128), lambda b, g: (layer, g, 0, 0)),
        pl.BlockSpec((None, 1, 512), lambda b, g: (layer, 0, g)),
    ]
    args = [proj, proj, proj, proj, conv_w, conv_w, conv_w, conv_b, conv_b, conv_b, dtb, alog, dvec]
    if has_h0:
        st_spec = pl.BlockSpec((None, None, SSD_R, SSD_P, SSD_N), lambda b, g: (b, layer, g, 0, 0))
        in_specs += [st_spec, st_spec]
        args += [h0[0], h0[1]]
    aliases = {}
    if dst is not None:
        in_specs.append(pl.BlockSpec(memory_space=pl.ANY))
        args.append(dst)
        aliases = {len(args) - 1: 0}
    out_specs = [pl.BlockSpec((seq, 512), lambda b, g: (rb + b, g))]
    out_shape = [jax.ShapeDtypeStruct((proj.shape[0], D), F32)]
    if out_state:
        so = pl.BlockSpec((None, SSD_R, SSD_P, SSD_N), lambda b, g: (b, g, 0, 0))
        out_specs += [so, so]
        out_shape += [jax.ShapeDtypeStruct((nb, SSD_H, SSD_P, SSD_N), F32)] * 2
    return pl.pallas_call(
        functools.partial(_ssd_kernel, seq=seq, has_h0=has_h0, out_state=out_state, has_dst=dst is not None),
        input_output_aliases=aliases,
        grid=(nb, SSD_G),
        in_specs=in_specs,
        out_specs=out_specs,
        out_shape=out_shape,
        scratch_shapes=[pltpu.VMEM((seq, 512), F32), pltpu.VMEM((seq, 128), F32),
                        pltpu.VMEM((seq, 128), F32), pltpu.VMEM((2, SSD_R // 2, 128, 128), F32)],
        compiler_params=_cp(("parallel", "parallel")),
        name="ssd_latent" if has_h0 else "ssd_ctx",
    )(*args)


def _att_kernel(*refs, seq, past, tq, latent, has_dst):
    it = iter(refs)
    q_ref, k_ref, v_ref, qg_ref, kg_ref = next(it), next(it), next(it), next(it), next(it)
    if latent:
        kc_ref, vc_ref, cos_ref, sa_ref, sb_ref = next(it), next(it), next(it), next(it), next(it)
    if has_dst:
        next(it)
    o_ref = next(it)
    if not latent:
        ko_ref, vo_ref = next(it), next(it)
    kb_ref, vb_ref = next(it), next(it)
    qi = pl.program_id(2)

    @pl.when(qi == 0)
    def _():
        kn = _rms(k_ref[...], kg_ref[...])
        v = v_ref[...]
        if latent:
            kn = _rope(kn, cos_ref[...], sa_ref[...], sb_ref[...])
            kb_ref[0:past, :] = kc_ref[...].astype(BF16)
            vb_ref[0:past, 0:ATT_D] = vc_ref[...].astype(BF16)
            kb_ref[past:past + seq, :] = kn.astype(BF16)
            vb_ref[past:past + seq, 0:ATT_D] = v.astype(BF16)
        else:
            ko_ref[...] = kn
            vo_ref[...] = v
            kb_ref[...] = kn.astype(BF16)
            vb_ref[:, 0:ATT_D] = v.astype(BF16)
        vb_ref[:, ATT_D:2 * ATT_D] = jnp.ones((past + seq, ATT_D), BF16)

    scale = ATT_D ** -0.5 * 1.4426950408889634
    kb = kb_ref[...]
    vb = vb_ref[...]
    if latent:
        q0 = pl.multiple_of(qi * tq, tq)
        cos = cos_ref[pl.ds(q0, tq), :]
        sa = sa_ref[pl.ds(q0, tq), :]
        sb = sb_ref[pl.ds(q0, tq), :]
    for r in range(ATT_R):
        qn = _rms(q_ref[:, r * ATT_D:(r + 1) * ATT_D], qg_ref[...])
        if latent:
            qn = _rope(qn, cos, sa, sb)
        s = _dot_nt((qn * scale).astype(BF16), kb)
        m = jnp.max(s, axis=-1, keepdims=True)
        p = jnp.exp2(s - m)
        ol = _dot(p.astype(BF16), vb)
        o = ol[:, 0:ATT_D] / ol[:, ATT_D:ATT_D + 1]
        o_ref[:, r * ATT_D:(r + 1) * ATT_D] = o.astype(BF16)


def _att(proj, row_off, nb, seq, qg, kg, layer, tq, ctx=None, rope=None, dst=None):
    latent = ctx is not None
    past = ctx[0].shape[3] if latent else 0
    nq = seq // tq
    rbq = row_off // tq
    rbs = row_off // seq
    in_specs = [
        pl.BlockSpec((tq, 512), lambda b, g, q: (rbq + b * nq + q, OFF_AQ // 512 + g)),
        pl.BlockSpec((seq, 128), lambda b, g, q: (rbs + b, OFF_AK // 128 + g)),
        pl.BlockSpec((seq, 128), lambda b, g, q: (rbs + b, OFF_AV // 128 + g)),
        pl.BlockSpec((None, 1, ATT_D), lambda b, g, q: (layer, 0, 0)),
        pl.BlockSpec((None, 1, ATT_D), lambda b, g, q: (layer, 0, 0)),
    ]
    args = [proj, proj, proj, qg, kg]
    if latent:
        cs = pl.BlockSpec((None, None, None, past, ATT_D), lambda b, g, q: (b, layer, g, 0, 0))
        ts = pl.BlockSpec((seq, ATT_D), lambda b, g, q: (0, 0))
        in_specs += [cs, cs, ts, ts, ts]
        args += [ctx[0], ctx[1], rope[0], rope[1], rope[2]]
    aliases = {}
    if dst is not None:
        in_specs.append(pl.BlockSpec(memory_space=pl.ANY))
        args.append(dst)
        aliases = {len(args) - 1: 0}
    out_specs = [pl.BlockSpec((tq, 512), lambda b, g, q: (rbq + b * nq + q, g))]
    out_shape = [jax.ShapeDtypeStruct((proj.shape[0], D), BF16)]
    if not latent:
        so = pl.BlockSpec((None, None, seq, ATT_D), lambda b, g, q: (b, g, 0, 0))
        out_specs += [so, so]
        out_shape += [jax.ShapeDtypeStruct((nb, ATT_G, seq, ATT_D), F32)] * 2
    return pl.pallas_call(
        functools.partial(_att_kernel, seq=seq, past=past, tq=tq, latent=latent, has_dst=dst is not None),
        input_output_aliases=aliases,
        grid=(nb, ATT_G, nq),
        in_specs=in_specs,
        out_specs=out_specs,
        out_shape=out_shape,
        scratch_shapes=[pltpu.VMEM((past + seq, ATT_D), BF16), pltpu.VMEM((past + seq, 2 * ATT_D), BF16)],
        compiler_params=_cp(("parallel", "parallel", "arbitrary")),
        name="att_latent" if latent else "att_ctx",
    )(*args)


def _ret_kernel(*refs, seq, latent, out_state, has_dst):
    it = iter(refs)
    q_ref, k_ref, v_ref, rg_ref, rd_ref, ng_ref = (next(it) for _ in range(6))
    if latent:
        cos_ref, sa_ref, sb_ref, s0f_ref, s0b_ref = (next(it) for _ in range(5))
    if has_dst:
        next(it)
    o_ref = next(it)
    if out_state:
        sout_refs = (next(it), next(it))
    of_ref, st_ref = next(it), next(it)

    nc = seq // CH
    ri = lax.broadcasted_iota(jnp.int32, (CH, CH), 0)
    ci = lax.broadcasted_iota(jnp.int32, (CH, CH), 1)
    pos = ri.astype(F32)
    diff = (ri - ci).astype(F32)
    scale = RET_K ** -0.5
    rd = rd_ref[...]
    lg_all = jnp.minimum(rd, 0.0) - jnp.log(1.0 + jnp.exp(-jnp.abs(rd)))

    for d in range(2):
        lg = lg_all[d:d + 1, :]
        if d == 0:
            dm = jnp.exp(jnp.where(ri >= ci, diff * lg, -jnp.inf))
            kte = jnp.exp((CH - 1.0 - pos) * lg)
            qfs = jnp.exp((pos + 1.0) * lg)
        else:
            dm = jnp.exp(jnp.where(ci > ri, -diff * lg, -jnp.inf))
            kte = jnp.exp(pos * lg)
            qfs = jnp.exp((CH - pos) * lg)
        cdec = jnp.exp(CH * lg)
        cdec2 = jnp.concatenate([cdec, cdec], axis=1)
        if latent:
            st_ref[...] = (s0f_ref if d == 0 else s0b_ref)[...]
        else:
            st_ref[...] = jnp.zeros_like(st_ref)

        def chunk(c, carry, d=d, dm=dm, kte=kte, qfs=qfs, cdec2=cdec2):
            ch = c if d == 0 else nc - 1 - c
            t0 = pl.multiple_of(ch * CH, CH)
            q = q_ref[pl.ds(t0, CH), :] * scale
            k = k_ref[pl.ds(t0, CH), :]
            if latent:
                cos = cos_ref[pl.ds(t0, CH), :]
                sa = sa_ref[pl.ds(t0, CH), :]
                sb = sb_ref[pl.ds(t0, CH), :]
                q = _rope(q, cos, sa, sb)
                k = _rope(k, cos, sa, sb)
            vb = v_ref[pl.ds(t0, CH), :].astype(BF16)
            sc = _dot_nt(q.astype(BF16), k.astype(BF16)) * dm
            st = st_ref[...]
            o = _dot(sc.astype(BF16), vb) + _dot((q * qfs).astype(BF16), st.astype(BF16))
            st_ref[...] = cdec2 * st + _dot((k * kte).T.astype(BF16), vb)
            if d == 0:
                of_ref[pl.ds(t0, CH), :] = o
            else:
                tot = of_ref[pl.ds(t0, CH), :] + o
                y = _silu(rg_ref[pl.ds(t0, CH), :]) * _rms(tot, ng_ref[...])
                o_ref[pl.ds(t0, CH), :] = y.astype(BF16)
            return carry

        lax.fori_loop(0, nc, chunk, 0, unroll=min(nc, 4))
        if out_state:
            sout_refs[d][...] = st_ref[...]


def _ret(proj, row_off, nb, seq, rd, ng, layer, rope=None, s0=None, out_state=False, dst=None):
    latent = s0 is not None
    rb = row_off // seq
    in_specs = [
        pl.BlockSpec((seq, RET_K), lambda b, h: (rb + b, OFF_RQ // RET_K + h)),
        pl.BlockSpec((seq, RET_K), lambda b, h: (rb + b, OFF_RK // RET_K + h)),
        pl.BlockSpec((seq, RET_V), lambda b, h: (rb + b, OFF_RV // RET_V + h)),
        pl.BlockSpec((seq, RET_V), lambda b, h: (rb + b, OFF_RG // RET_V + h)),
        pl.BlockSpec((None, None, 2, 128), lambda b, h: (layer, h, 0, 0)),
        pl.BlockSpec((None, 1, RET_V), lambda b, h: (layer, 0, h)),
    ]
    args = [proj, proj, proj, proj, rd, ng]
    if latent:
        ts = pl.BlockSpec((seq, RET_K), lambda b, h: (0, 0))
        ss = pl.BlockSpec((None, None, None, RET_K, RET_V), lambda b, h: (b, layer, h, 0, 0))
        in_specs += [ts, ts, ts, ss, ss]
        args += [rope[0], rope[1], rope[2], s0[0], s0[1]]
    aliases = {}
    if dst is not None:
        in_specs.append(pl.BlockSpec(memory_space=pl.ANY))
        args.append(dst)
        aliases = {len(args) - 1: 0}
    out_specs = [pl.BlockSpec((seq, RET_V), lambda b, h: (rb + b, h))]
    out_shape = [jax.ShapeDtypeStruct((proj.shape[0], D), BF16)]
    if out_state:
        so = pl.BlockSpec((None, None, RET_K, RET_V), lambda b, h: (b, h, 0, 0))
        out_specs += [so, so]
        out_shape += [jax.ShapeDtypeStruct((nb, RET_H, RET_K, RET_V), F32)] * 2
    return pl.pallas_call(
        functools.partial(_ret_kernel, seq=seq, latent=latent, out_state=out_state, has_dst=dst is not None),
        input_output_aliases=aliases,
        grid=(nb, RET_H),
        in_specs=in_specs,
        out_specs=out_specs,
        out_shape=out_shape,
        scratch_shapes=[pltpu.VMEM((seq, RET_V), F32), pltpu.VMEM((RET_K, RET_V), F32)],
        compiler_params=_cp(("parallel", "parallel")),
        name="ret_latent" if latent else "ret_ctx",
    )(*args)


def _merge_kernel(ys_ref, z_ref, ng_ref, ya_ref, yr_ref, g0_ref, g1_ref, g2_ref, wb_ref, o_ref, y0_ref):
    @pl.when(pl.program_id(1) == 0)
    def _():
        y = ys_ref[...] * _silu(z_ref[...])
        y0_ref[...] = _rms(y, ng_ref[...]).astype(BF16)

    acc = _sigmoid(g0_ref[...]) * _dot(y0_ref[...], wb_ref[0])
    acc += _sigmoid(g1_ref[...]) * _dot(ya_ref[...], wb_ref[1])
    acc += _sigmoid(g2_ref[...]) * _dot(yr_ref[...], wb_ref[2])
    o_ref[...] = acc.astype(BF16)


def _merge(y_ssd, proj, ssd_ng, y_att, y_ret, w_branch, layer, tm=512, tn=512):
    t = y_ssd.shape[0]
    gl = OFF_GL // tn
    nj = D // tn
    return pl.pallas_call(
        _merge_kernel,
        grid=(t // tm, nj),
        in_specs=[pl.BlockSpec((tm, D), lambda i, j: (i, 0)),
                  pl.BlockSpec((tm, D), lambda i, j: (i, OFF_Z // D)),
                  pl.BlockSpec((None, 1, D), lambda i, j: (layer, 0, 0)),
                  pl.BlockSpec((tm, D), lambda i, j: (i, 0)),
                  pl.BlockSpec((tm, D), lambda i, j: (i, 0)),
                  pl.BlockSpec((tm, tn), lambda i, j: (i, gl + j)),
                  pl.BlockSpec((tm, tn), lambda i, j: (i, gl + nj + j)),
                  pl.BlockSpec((tm, tn), lambda i, j: (i, gl + 2 * nj + j)),
                  pl.BlockSpec((None, 3, D, tn), lambda i, j: (layer, 0, 0, j))],
        out_specs=pl.BlockSpec((tm, tn), lambda i, j: (i, j)),
        out_shape=jax.ShapeDtypeStruct((t, D), BF16),
        scratch_shapes=[pltpu.VMEM((tm, D), BF16)],
        compiler_params=_cp(("parallel", "arbitrary")),
        name="merge",
    )(y_ssd, proj, ssd_ng, y_att, y_ret, proj, proj, proj, w_branch)


def _outproj_kernel(m_ref, w_ref, x_ref, g1_ref, ng_ref, sh_ref, sc_ref, wr_ref,
                    x1_ref, h2_ref, aff_ref, afft_ref):
    x1 = x_ref[...] + g1_ref[...] * _dot(m_ref[...], w_ref[...])
    x1_ref[...] = x1
    h = _rms(x1, ng_ref[...]) * (1.0 + sc_ref[...]) + sh_ref[...]
    hb = h.astype(BF16)
    h2_ref[...] = hb
    hl = (h - hb.astype(F32)).astype(BF16)
    logits = _dot(hb, wr_ref[0]) + _dot(hl, wr_ref[0]) + _dot(hb, wr_ref[1])
    lane = lax.broadcasted_iota(jnp.int32, logits.shape, 1)
    logits = jnp.where(lane < N_EXP, logits, -jnp.inf)
    e = jnp.exp(logits - jnp.max(logits, axis=-1, keepdims=True))
    aff = e / jnp.sum(e, axis=-1, keepdims=True)
    aff_ref[...] = aff
    afft_ref[...] = aff.T[0:N_EXP, :]


def _outproj(merged, w_out, x, mods, norm_g, w_router, layer, mod_row, tm=512):
    t = x.shape[0]

    def ms(col):
        return pl.BlockSpec((None, None, 1, D), lambda i: (layer, mod_row(i, tm), 0, col))

    return pl.pallas_call(
        _outproj_kernel,
        grid=(t // tm,),
        in_specs=[pl.BlockSpec((tm, D), lambda i: (i, 0)),
                  pl.BlockSpec((None, D, D), lambda i: (layer, 0, 0)),
                  pl.BlockSpec((tm, D), lambda i: (i, 0)),
                  ms(2),
                  pl.BlockSpec((None, 1, D), lambda i: (layer, 0, 0)),
                  ms(3), ms(4),
                  pl.BlockSpec((None, 2, D, 128), lambda i: (layer, 0, 0, 0))],
        out_specs=[pl.BlockSpec((tm, D), lambda i: (i, 0)),
                   pl.BlockSpec((tm, D), lambda i: (i, 0)),
                   pl.BlockSpec((tm, 128), lambda i: (i, 0)),
                   pl.BlockSpec((N_EXP, tm), lambda i: (0, i))],
        out_shape=[jax.ShapeDtypeStruct((t, D), F32),
                   jax.ShapeDtypeStruct((t, D), BF16),
                   jax.ShapeDtypeStruct((t, 128), F32),
                   jax.ShapeDtypeStruct((N_EXP, t), F32)],
        compiler_params=_cp(("parallel",)),
        name="out_proj",
    )(merged, w_out, x, mods, norm_g, mods, mods, w_router)


SUB = 128
WIN = SUB + 16


def _select_kernel(a_ref, post_ref, postok_ref, st_ref, cnt_ref, *, groups):
    ri = lax.broadcasted_iota(jnp.int32, (SUB, SUB), 0)
    ci = lax.broadcasted_iota(jnp.int32, (SUB, SUB), 1)
    before = jnp.where(ri < ci, 1.0, 0.0).astype(BF16)
    lane = lax.broadcasted_iota(jnp.int32, (N_EXP, SUB), 1)
    st_ref[...] = jnp.zeros_like(st_ref)
    for lo, n, cap, base in groups:
        bits = pltpu.bitcast(a_ref[:, lo:lo + n], jnp.int32)

        def search(i, prefix, bits=bits, cap=cap):
            cand = prefix | jnp.left_shift(jnp.int32(1), 30 - i)
            cnt = jnp.sum(jnp.where(bits >= cand, 1.0, 0.0), axis=1, keepdims=True)
            return jnp.where(cnt >= cap, cand, prefix)

        zero = pltpu.bitcast(jnp.minimum(jnp.min(a_ref[:, lo:lo + n], axis=1, keepdims=True), 0.0), jnp.int32)
        tau = lax.fori_loop(0, 31, search, zero)
        need = cap - jnp.sum(jnp.where(bits > tau, 1.0, 0.0), axis=1, keepdims=True)

        cnt_ref[...] = jnp.zeros_like(cnt_ref)

        def block(b, carry, lo=lo, base=base, tau=tau, need=need):
            ceq = cnt_ref[0][:, 0:1]
            csel = cnt_ref[1][:, 0:1]
            t0 = pl.multiple_of(lo + b * SUB, SUB)
            bb = pltpu.bitcast(a_ref[:, pl.ds(t0, SUB)], jnp.int32)
            eq = jnp.where(bb == tau, 1.0, 0.0)
            rank_eq = ceq + _dot(eq.astype(BF16), before)
            sel = jnp.where((bb > tau) | ((bb == tau) & (rank_eq < need)), 1.0, 0.0)
            pos = csel + _dot(sel.astype(BF16), before) + base
            pm = jnp.where(sel > 0.0, pos, -1.0)
            post_ref[:, pl.ds(t0, SUB)] = pm
            full = jnp.concatenate([pm, jnp.full((SUB - N_EXP, SUB), -1.0, F32)], axis=0)
            postok_ref[pl.ds(t0, SUB), :] = full.T
            st_ref[...] = jnp.where(lane == lo // SUB + b, (csel + base).astype(jnp.int32), st_ref[...])
            cnt_ref[0] = jnp.broadcast_to(ceq + jnp.sum(eq, axis=1, keepdims=True), (N_EXP, SUB))
            cnt_ref[1] = jnp.broadcast_to(csel + jnp.sum(sel, axis=1, keepdims=True), (N_EXP, SUB))
            return carry

        lax.fori_loop(0, n // SUB, block, 0)


def _select(afft, groups):
    t = afft.shape[1]
    assert t // SUB <= SUB
    return pl.pallas_call(
        functools.partial(_select_kernel, groups=groups),
        out_shape=[jax.ShapeDtypeStruct((N_EXP, t), F32),
                   jax.ShapeDtypeStruct((t, SUB), F32),
                   jax.ShapeDtypeStruct((N_EXP, SUB), jnp.int32)],
        scratch_shapes=[pltpu.VMEM((2, N_EXP, SUB), F32)],
        compiler_params=pltpu.CompilerParams(vmem_limit_bytes=VMEM_LIMIT),
        name="select",
    )(afft)


def _ffn_kernel(st_ref, h_ref, pos_ref, w1_ref, w3_ref, w2_ref, o_ref, stage_ref, hid_ref,
                *, n_e, nbd, nf, n2, cap, tbd, tf):
    r = pl.program_id(0)
    s = pl.program_id(1)

    @pl.when((r < n_e) & (s < nbd))
    def _():
        slot = r % 2

        @pl.when(s == 0)
        def _():
            stage_ref[slot] = jnp.zeros(stage_ref.shape[1:], BF16)

        srow = lax.broadcasted_iota(jnp.int32, (WIN, SUB), 0).astype(F32)
        for k in range(tbd // SUB):
            off16 = pl.multiple_of((st_ref[r, s * (tbd // SUB) + k] // 16) * 16, 16)
            rel = pos_ref[:, k * SUB:(k + 1) * SUB] - off16.astype(F32)
            onehot = jnp.where(srow == rel, 1.0, 0.0).astype(BF16)
            rows = _dot(onehot, h_ref[k * SUB:(k + 1) * SUB, :]).astype(BF16)
            stage_ref[slot, pl.ds(off16, WIN), :] += rows

    @pl.when((r > 0) & (s < nf))
    def _():
        x = stage_ref[(r + 1) % 2, 0:cap, :]
        hid = _silu(_dot(x, w1_ref[...].astype(BF16))) * _dot(x, w3_ref[...].astype(BF16))
        hid_ref[:, pl.ds(pl.multiple_of(s * tf, tf), tf)] = hid.astype(BF16)

    @pl.when((r > 0) & (s >= nf) & (s < nf + n2))
    def _():
        o_ref[...] = _dot(hid_ref[...], w2_ref[...].astype(BF16)).astype(BF16)


def _ffn(starts, h2, post, w1, w3, w2, layer, cap, tf=256, tn=256):
    t = h2.shape[0]
    n_e, ff = w1.shape[1], w1.shape[-1]
    nf = ff // tf
    n2 = D // tn
    tbd = max(SUB, (t // (nf + n2)) // SUB * SUB)
    while t % tbd:
        tbd -= SUB
    nbd = t // tbd
    steps = max(nbd, nf + n2)

    def tok(r, s):
        return jnp.where(r < n_e, jnp.minimum(s, nbd - 1), nbd - 1)

    def prev(r):
        return jnp.maximum(r - 1, 0)

    def fcol(s):
        return jnp.minimum(s, nf - 1)

    def ocol(r, s):
        return jnp.where(r > 0, jnp.clip(s - nf, 0, n2 - 1), 0)

    grid_spec = pltpu.PrefetchScalarGridSpec(
        num_scalar_prefetch=1,
        grid=(n_e + 1, steps),
        in_specs=[pl.BlockSpec((tbd, D), lambda r, s, st: (tok(r, s), 0)),
                  pl.BlockSpec((None, 1, tbd), lambda r, s, st: (jnp.minimum(r, n_e - 1), 0, tok(r, s))),
                  pl.BlockSpec((None, None, D, tf), lambda r, s, st: (layer, prev(r), 0, fcol(s))),
                  pl.BlockSpec((None, None, D, tf), lambda r, s, st: (layer, prev(r), 0, fcol(s))),
                  pl.BlockSpec((None, None, ff, tn), lambda r, s, st: (layer, prev(r), 0, ocol(r, s)))],
        out_specs=pl.BlockSpec((None, cap, tn), lambda r, s, st: (prev(r), 0, ocol(r, s))),
        scratch_shapes=[pltpu.VMEM((2, cap + WIN, D), BF16), pltpu.VMEM((cap, ff), BF16)],
    )
    return pl.pallas_call(
        functools.partial(_ffn_kernel, n_e=n_e, nbd=nbd, nf=nf, n2=n2, cap=cap, tbd=tbd, tf=tf),
        grid_spec=grid_spec,
        out_shape=jax.ShapeDtypeStruct((n_e, cap, D), BF16),
        compiler_params=_cp(("arbitrary", "arbitrary")),
        name="expert_ffn",
    )(starts, h2, post.reshape(n_e, 1, t), w1, w3, w2)


def _combine_kernel(st_ref, ye_hbm, postok_ref, aff_ref, x1_ref, g2_ref, o_ref, win_ref, sem, *, cap):
    b = pl.program_id(0)
    nb = pl.num_programs(0)

    def first_slot(bb, e):
        return pl.multiple_of(jnp.minimum((st_ref[e, bb] // 16) * 16, cap - WIN), 16)

    def window_copy(bb, slot, e):
        return pltpu.make_async_copy(ye_hbm.at[e, pl.ds(first_slot(bb, e), WIN), :],
                                     win_ref.at[slot, pl.ds(e * WIN, WIN), :], sem.at[slot])

    @pl.when(b == 0)
    def _():
        tail = jnp.zeros((2 * SUB - WIN, D), BF16)
        win_ref[0, N_EXP * WIN:, :] = tail
        win_ref[1, N_EXP * WIN:, :] = tail
        for e in range(N_EXP):
            window_copy(0, 0, e).start()

    @pl.when(b + 1 < nb)
    def _():
        for e in range(N_EXP):
            window_copy(b + 1, (b + 1) % 2, e).start()

    slot = b % 2
    for e in range(N_EXP):
        window_copy(b, slot, e).wait()

    lane = lax.broadcasted_iota(jnp.int32, (SUB, 2 * SUB), 1).astype(F32)
    acc = jnp.zeros((SUB, D), F32)
    for e in range(N_EXP):
        rel = postok_ref[:, e:e + 1] - first_slot(b, e).astype(F32)
        gated = jnp.where(lane == rel, aff_ref[:, e:e + 1], 0.0).astype(BF16)
        acc += _dot(gated, win_ref[slot, pl.ds(e * WIN, 2 * SUB), :])
    o_ref[...] = x1_ref[...] + g2_ref[...] * acc


def _combine(starts, ye, postok, aff, x1, mods, layer, mod_row):
    t = x1.shape[0]
    n_e, cap, _ = ye.shape
    grid_spec = pltpu.PrefetchScalarGridSpec(
        num_scalar_prefetch=1,
        grid=(t // SUB,),
        in_specs=[pl.BlockSpec(memory_space=pl.ANY),
                  pl.BlockSpec((SUB, SUB), lambda b, st: (b, 0)),
                  pl.BlockSpec((SUB, SUB), lambda b, st: (b, 0)),
                  pl.BlockSpec((SUB, D), lambda b, st: (b, 0)),
                  pl.BlockSpec((None, None, 1, D), lambda b, st: (layer, mod_row(b, SUB), 0, 5))],
        out_specs=pl.BlockSpec((SUB, D), lambda b, st: (b, 0)),
        scratch_shapes=[pltpu.VMEM((2, N_EXP * WIN + 2 * SUB - WIN, D), BF16),
                        pltpu.SemaphoreType.DMA((2,))],
    )
    return pl.pallas_call(
        functools.partial(_combine_kernel, cap=cap),
        grid_spec=grid_spec,
        out_shape=jax.ShapeDtypeStruct((t, D), F32),
        compiler_params=_cp(("arbitrary",)),
        name="combine",
    )(starts, ye, postok, aff, x1, mods)


def _rope_tables(n_tokens):
    rows_n = n_tokens // GRID_W
    row = jnp.repeat(jnp.arange(rows_n, dtype=F32), GRID_W)
    col = jnp.tile(jnp.arange(GRID_W, dtype=F32), rows_n)
    n_freq = ATT_D // 4
    inv = ROPE_THETA ** (-jnp.arange(n_freq, dtype=F32) / n_freq)
    ang_r = row[:, None] * inv
    ang_c = col[:, None] * inv
    zeros = jnp.zeros_like(ang_r)
    cos = jnp.concatenate([jnp.cos(ang_r)] * 2 + [jnp.cos(ang_c)] * 2, axis=1)
    sa = jnp.concatenate([-jnp.sin(ang_r), zeros, -jnp.sin(ang_c), zeros], axis=1)
    sb = jnp.concatenate([zeros, jnp.sin(ang_r), zeros, jnp.sin(ang_c)], axis=1)
    return cos, sa, sb


def _group_dt_lanes(v):
    depth = v.shape[0]
    v = v.reshape(depth, 2, SSD_G, SSD_R).transpose(0, 2, 1, 3).reshape(depth, SSD_G, 1, 2 * SSD_R)
    return jnp.pad(v, ((0, 0), (0, 0), (0, 0), (0, 128 - 2 * SSD_R)))


def _prep_w_in(w_in):
    depth = w_in.shape[0]
    dt = w_in[:, :, ORIG_DT:ORIG_AFTER_DT].reshape(depth, D, 2, SSD_G, SSD_R)
    dt = dt.transpose(0, 1, 3, 2, 4).reshape(depth, D, SSD_G, 2 * SSD_R)
    dt = jnp.pad(dt, ((0, 0), (0, 0), (0, 0), (0, 128 - 2 * SSD_R))).reshape(depth, D, SSD_G * 128)
    w = jnp.concatenate([w_in[:, :, :ORIG_DT], w_in[:, :, ORIG_AFTER_DT:], dt], axis=2)
    return w.astype(BF16)


def kernel(x_prompt, x_sample, cache_attn_k, cache_attn_v, state_ssd_fwd, state_ssd_bwd, state_ret_fwd, state_ret_bwd, c, c_ctx, norm1_g, norm2_g, w_ada, b_ada, w_in, ssd_conv_w, ssd_conv_b, ssd_a_log, ssd_dt_bias, ssd_d, ssd_norm_g, q_norm_g, k_norm_g, ret_decay, ret_norm_g, w_branch, w_out, w_router, w_e1, w_e3, w_e2):
    depth = w_in.shape[0]
    pb, pseq, _ = x_prompt.shape
    sb, sseq, _ = x_sample.shape
    n_p = pb * pseq
    n_s = sb * sseq
    assert n_p % 1024 == 0 and sseq % 1024 == 0 and 1 + sb <= 8

    def mod_row(i, tm):
        return jnp.where(i < n_p // tm, 0, 1 + (i * tm - n_p) // sseq)

    w_in_b = _prep_w_in(w_in)
    w_branch_b = w_branch.astype(BF16)
    w_out_b = w_out.astype(BF16)
    cap_p = CAP_FACTOR * n_p // N_EXP
    cap_s = CAP_FACTOR * n_s // N_EXP
    groups = ((0, n_p, cap_p, 0), (n_p, n_s, cap_s, cap_p))
    w_router_f = jnp.pad(w_router, ((0, 0), (0, 0), (0, 128 - N_EXP)))
    w_router_hi = w_router_f.astype(BF16)
    w_router_p = jnp.stack([w_router_hi, (w_router_f - w_router_hi.astype(F32)).astype(BF16)], axis=1)
    dtb = _group_dt_lanes(ssd_dt_bias)
    alog = _group_dt_lanes(ssd_a_log)
    dvec = jnp.repeat(ssd_d, SSD_P, axis=1).reshape(depth, 1, D)
    rd = jnp.broadcast_to(ret_decay.transpose(0, 2, 1)[..., None], (depth, RET_H, 2, 128))
    rope = _rope_tables(sseq)
    n1g = norm1_g.reshape(depth, 1, D)
    n2g = norm2_g.reshape(depth, 1, D)
    ssd_ng = ssd_norm_g.reshape(depth, 1, D)
    ret_ng = ret_norm_g.reshape(depth, 1, D)
    qg = q_norm_g.reshape(depth, 1, ATT_D)
    kg = k_norm_g.reshape(depth, 1, ATT_D)

    cond8 = jnp.zeros((8, D), F32).at[0].set(c_ctx).at[1:1 + sb].set(c)
    mods = _ada(cond8, w_ada, b_ada).reshape(depth, 8, 1, 6 * D)

    x = jnp.concatenate([x_prompt.reshape(n_p, D), x_sample.reshape(n_s, D)], axis=0)
    new_k, new_v, new_hf, new_hb, new_sf, new_sb = [], [], [], [], [], []
    y_ssd = jnp.zeros((n_p + n_s, D), F32)
    y_att = jnp.zeros((n_p + n_s, D), BF16)
    y_ret = jnp.zeros((n_p + n_s, D), BF16)
    for l in range(depth):
        proj = _inproj(x, n1g, mods, w_in_b, l, mod_row)

        ys_p, hf, hb = _ssd(proj, 0, pb, pseq, ssd_conv_w, ssd_conv_b.reshape(depth, 1, -1), dtb, alog, dvec,
                            l, out_state=True, dst=y_ssd)
        (y_ssd,) = _ssd(proj, n_p, sb, sseq, ssd_conv_w, ssd_conv_b.reshape(depth, 1, -1), dtb, alog, dvec,
                        l, h0=(state_ssd_fwd, state_ssd_bwd), dst=ys_p)
        ya_p, kn, vn = _att(proj, 0, pb, pseq, qg, kg, l, tq=pseq, dst=y_att)
        (y_att,) = _att(proj, n_p, sb, sseq, qg, kg, l, tq=256, ctx=(cache_attn_k, cache_attn_v), rope=rope,
                        dst=ya_p)
        yr_p, sf, sbk = _ret(proj, 0, pb, pseq, rd, ret_ng, l, out_state=True, dst=y_ret)
        (y_ret,) = _ret(proj, n_p, sb, sseq, rd, ret_ng, l, rope=rope, s0=(state_ret_fwd, state_ret_bwd),
                        dst=yr_p)
        new_k.append(kn), new_v.append(vn), new_hf.append(hf), new_hb.append(hb)
        new_sf.append(sf), new_sb.append(sbk)

        merged = _merge(y_ssd, proj, ssd_ng, y_att, y_ret, w_branch_b, l)
        x1, h2, aff, afft = _outproj(merged, w_out_b, x, mods, n2g, w_router_p, l, mod_row)
        post, postok, starts = _select(afft, groups)
        ye = _ffn(starts, h2, post, w_e1, w_e3, w_e2, l, cap_p + cap_s)
        x = _combine(starts, ye, postok, aff, x1, mods, l, mod_row)

    y_prompt = x[:n_p].reshape(pb, pseq, D)
    y_sample = x[n_p:].reshape(sb, sseq, D)
    return (y_prompt, y_sample, jnp.stack(new_k, axis=1), jnp.stack(new_v, axis=1),
            jnp.stack(new_hf, axis=1), jnp.stack(new_hb, axis=1),
            jnp.stack(new_sf, axis=1), jnp.stack(new_sb, axis=1))
```

```python
import functools

import jax
import jax.numpy as jnp
from jax import lax
from jax.experimental import pallas as pl
from jax.experimental.pallas import tpu as pltpu

F32 = jnp.float32
BF16 = jnp.bfloat16
EPS = 1e-6
ROPE_THETA = 10000.0
GRID_W = 64

D = 2048
CH = 128
SSD_P = 64
SSD_N = 128
SSD_G = 4
SSD_H = 32
SSD_R = SSD_H // SSD_G
ATT_D = 128
ATT_H = 16
ATT_G = 4
ATT_R = ATT_H // ATT_G
RET_H = 8
RET_K = 128
RET_V = 256
N_EXP = 16
CAP_FACTOR = 2

OFF_Z = 0
OFF_X = 2048
OFF_B = 4096
OFF_C = 4608
OFF_AQ = 5120
OFF_AK = 7168
OFF_AV = 7680
OFF_RQ = 8192
OFF_RK = 9216
OFF_RV = 10240
OFF_RG = 12288
OFF_GL = 14336
OFF_DT = 20480
NW = OFF_DT + SSD_G * 128
ORIG_DT = 5120
ORIG_AFTER_DT = 5184
ORIG_W = 20544

VMEM_LIMIT = 56 * 1024 * 1024


def _cp(sem):
    return pltpu.CompilerParams(dimension_semantics=sem, vmem_limit_bytes=VMEM_LIMIT)


def _sigmoid(x):
    return 1.0 / (1.0 + jnp.exp(-x))


def _silu(x):
    return x * _sigmoid(x)


def _softplus(x):
    return jnp.maximum(x, 0.0) + jnp.log(1.0 + jnp.exp(-jnp.abs(x)))


def _rms(x, g):
    ms = jnp.mean(x * x, axis=-1, keepdims=True)
    return x * lax.rsqrt(ms + EPS) * g


def _dot(a, b):
    return jnp.dot(a, b, preferred_element_type=F32)


def _dot_nt(a, b):
    return lax.dot_general(a, b, (((1,), (1,)), ((), ())), preferred_element_type=F32)


def _rope(x, cos, sa, sb):
    return x * cos + pltpu.roll(x, 96, 1) * sa + pltpu.roll(x, 32, 1) * sb


def _ada_kernel(c_ref, w_ref, b_ref, o_ref):
    s = _silu(c_ref[...]).astype(BF16)
    o_ref[...] = _dot(s, w_ref[...].astype(BF16)) + b_ref[...]


def _ada(cond8, w_ada, b_ada):
    depth, _, n6 = w_ada.shape
    tn = 1024
    return pl.pallas_call(
        _ada_kernel,
        grid=(depth, n6 // tn),
        in_specs=[pl.BlockSpec((8, D), lambda l, j: (0, 0)),
                  pl.BlockSpec((None, D, tn), lambda l, j: (l, 0, j)),
                  pl.BlockSpec((None, 1, tn), lambda l, j: (l, 0, j))],
        out_specs=pl.BlockSpec((None, 8, tn), lambda l, j: (l, 0, j)),
        out_shape=jax.ShapeDtypeStruct((depth, 8, n6), F32),
        compiler_params=_cp(("parallel", "parallel")),
        name="ada_mod",
    )(cond8, w_ada, b_ada.reshape(depth, 1, n6))


def _inproj_kernel(x_ref, g_ref, sh_ref, sc_ref, w_ref, o_ref, hn_ref):
    @pl.when(pl.program_id(1) == 0)
    def _():
        h = _rms(x_ref[...], g_ref[...])
        hn_ref[...] = (h * (1.0 + sc_ref[...]) + sh_ref[...]).astype(BF16)

    o_ref[...] = _dot(hn_ref[...], w_ref[...])


def _inproj(x, norm_g, mods, w_in, layer, mod_row, tm=1024, tn=512):
    t = x.shape[0]
    return pl.pallas_call(
        _inproj_kernel,
        grid=(t // tm, NW // tn),
        in_specs=[pl.BlockSpec((tm, D), lambda i, j: (i, 0)),
                  pl.BlockSpec((None, 1, D), lambda i, j: (layer, 0, 0)),
                  pl.BlockSpec((None, None, 1, D), lambda i, j: (layer, mod_row(i, tm), 0, 0)),
                  pl.BlockSpec((None, None, 1, D), lambda i, j: (layer, mod_row(i, tm), 0, 1)),
                  pl.BlockSpec((None, D, tn), lambda i, j: (layer, 0, j))],
        out_specs=pl.BlockSpec((tm, tn), lambda i, j: (i, j)),
        out_shape=jax.ShapeDtypeStruct((t, NW), F32),
        scratch_shapes=[pltpu.VMEM((tm, D), BF16)],
        compiler_params=_cp(("parallel", "arbitrary")),
        name="in_proj",
    )(x, norm_g, mods, mods, w_in)


def _ssd_kernel(*refs, seq, has_h0, out_state, has_dst):
    it = iter(refs)
    x_ref, b_ref, c_ref, dt_ref = next(it), next(it), next(it), next(it)
    cwx_ref, cwb_ref, cwc_ref = next(it), next(it), next(it)
    cbx_ref, cbb_ref, cbc_ref = next(it), next(it), next(it)
    dtb_ref, alog_ref, dvec_ref = next(it), next(it), next(it)
    h0_refs = (next(it), next(it)) if has_h0 else None
    if has_dst:
        next(it)
    y_ref = next(it)
    hout_refs = (next(it), next(it)) if out_state else None
    xs_ref, bs_ref, cs_ref, st_ref = next(it), next(it), next(it), next(it)

    nc = seq // CH
    row1 = lax.broadcasted_iota(jnp.int32, (CH, 1), 0)

    def conv_chunk(c, carry):
        t0 = pl.multiple_of(c * CH, CH)
        pidx = pl.multiple_of(jnp.maximum(t0 - 8, 0), 8)
        nidx = pl.multiple_of(jnp.minimum(t0 + CH, seq - 8), 8)

        def conv(src, w_ref, bias_ref, dst):
            cur = src[pl.ds(t0, CH), :]
            prev_row = jnp.where(c > 0, src[pl.ds(pidx, 8), :][7:8, :], 0.0)
            next_row = jnp.where(c < nc - 1, src[pl.ds(nidx, 8), :][0:1, :], 0.0)
            xm = jnp.where(row1 == 0, prev_row, pltpu.roll(cur, 1, 0))
            xp = jnp.where(row1 == CH - 1, next_row, pltpu.roll(cur, CH - 1, 0))
            w = w_ref[...]
            v = xm * w[0:1, :] + cur * w[1:2, :] + xp * w[2:3, :] + bias_ref[...]
            v = _silu(v)
            dst[pl.ds(t0, CH), :] = v
            return v

        xs = conv(x_ref, cwx_ref, cbx_ref, xs_ref)
        y_ref[pl.ds(t0, CH), :] = dvec_ref[...] * xs
        conv(b_ref, cwb_ref, cbb_ref, bs_ref)
        conv(c_ref, cwc_ref, cbc_ref, cs_ref)
        return carry

    lax.fori_loop(0, nc, conv_chunk, 0)

    ri = lax.broadcasted_iota(jnp.int32, (CH, CH), 0)
    ci = lax.broadcasted_iota(jnp.int32, (CH, CH), 1)
    tri = (ri >= ci).astype(F32)
    left = ci < SSD_P
    left_row = ci[0:1, :] < SSD_P
    dtb = dtb_ref[...]
    aneg = -jnp.exp(alog_ref[...])

    for d in range(2):
        for pair in range(SSD_R // 2):
            if has_h0:
                st_ref[d, pair] = h0_refs[d][2 * pair:2 * pair + 2].reshape(2 * SSD_P, SSD_N).T
            else:
                st_ref[d, pair] = jnp.zeros((SSD_N, 2 * SSD_P), F32)

    def scan_one(d, ch):
        mask = (ri >= ci) if d == 0 else (ci >= ri)
        t0 = pl.multiple_of(ch * CH, CH)
        dtv = _softplus(dt_ref[pl.ds(t0, CH), :] + dtb)
        a = dtv * aneg
        pref = jnp.dot(tri, a, precision=lax.Precision.HIGHEST, preferred_element_type=F32)
        tot = pref[CH - 1:CH, :]
        s = pref if d == 0 else tot - pref + a
        cd = jnp.exp(tot)
        s_t = s.T
        dt_t = dtv.T
        bc = bs_ref[pl.ds(t0, CH), :]
        bc_t = bc.T
        bcb = bc.astype(BF16)
        ccb = cs_ref[pl.ds(t0, CH), :].astype(BF16)
        cb = _dot_nt(ccb, bcb)
        for pair in range(SSD_R // 2):
            la = d * SSD_R + 2 * pair
            lb = la + 1
            x2 = xs_ref[pl.ds(t0, CH), pair * 128:(pair + 1) * 128]
            x_a = jnp.where(left, x2, 0.0).astype(BF16)
            x_b = jnp.where(left, 0.0, x2).astype(BF16)
            sa = jnp.broadcast_to(s[:, la:la + 1], (CH, CH))
            sb = jnp.broadcast_to(s[:, lb:lb + 1], (CH, CH))
            sa_t, sb_t = s_t[la:la + 1, :], s_t[lb:lb + 1, :]
            dta_t, dtb_t = dt_t[la:la + 1, :], dt_t[lb:lb + 1, :]
            w_a = (cb * jnp.exp(jnp.where(mask, sa - sa_t, -jnp.inf)) * dta_t).astype(BF16)
            w_b = (cb * jnp.exp(jnp.where(mask, sb - sb_t, -jnp.inf)) * dtb_t).astype(BF16)
            st = st_ref[d, pair]
            y = _dot(w_a, x_a) + _dot(w_b, x_b)
            y = y + _dot(ccb, st.astype(BF16)) * jnp.exp(jnp.where(left, sa, sb))
            g_a = jnp.exp(tot[:, la:la + 1] - sa_t) * dta_t
            g_b = jnp.exp(tot[:, lb:lb + 1] - sb_t) * dtb_t
            new = _dot((bc_t * g_a).astype(BF16), x_a) + _dot((bc_t * g_b).astype(BF16), x_b)
            cd2 = jnp.where(left_row, cd[:, la:la + 1], cd[:, lb:lb + 1])
            st_ref[d, pair] = cd2 * st + new
            y_ref[pl.ds(t0, CH), pair * 128:(pair + 1) * 128] += y

    def scan_chunk(c, carry):
        scan_one(0, c)
        scan_one(1, nc - 1 - c)
        return carry

    lax.fori_loop(0, nc, scan_chunk, 0)
    if out_state:
        for d in range(2):
            for pair in range(SSD_R // 2):
                hout_refs[d][2 * pair:2 * pair + 2] = st_ref[d, pair].T.reshape(2, SSD_P, SSD_N)


def _ssd(proj, row_off, nb, seq, conv_w, conv_b, dtb, alog, dvec, layer, h0=None, out_state=False, dst=None):
    rb = row_off // seq
    has_h0 = h0 is not None
    in_specs = [
        pl.BlockSpec((seq, 512), lambda b, g: (rb + b, OFF_X // 512 + g)),
        pl.BlockSpec((seq, 128), lambda b, g: (rb + b, OFF_B // 128 + g)),
        pl.BlockSpec((seq, 128), lambda b, g: (rb + b, OFF_C // 128 + g)),
        pl.BlockSpec((seq, 128), lambda b, g: (rb + b, OFF_DT // 128 + g)),
        pl.BlockSpec((None, 3, 512), lambda b, g: (layer, 0, g)),
        pl.BlockSpec((None, 3, 128), lambda b, g: (layer, 0, D // 128 + g)),
        pl.BlockSpec((None, 3, 128), lambda b, g: (layer, 0, D // 128 + SSD_G + g)),
        pl.BlockSpec((None, 1, 512), lambda b, g: (layer, 0, g)),
        pl.BlockSpec((None, 1, 128), lambda b, g: (layer, 0, D // 128 + g)),
        pl.BlockSpec((None, 1, 128), lambda b, g: (layer, 0, D // 128 + SSD_G + g)),
        pl.BlockSpec((None, None, 1, 128), lambda b, g: (layer, g, 0, 0)),
        pl.BlockSpec((None, None, 1, 128), lambda b, g: (layer, g, 0, 0)),
        pl.BlockSpec((None, 1, 512), lambda b, g: (layer, 0, g)),
    ]
    args = [proj, proj, proj, proj, conv_w, conv_w, conv_w, conv_b, conv_b, conv_b, dtb, alog, dvec]
    if has_h0:
        st_spec = pl.BlockSpec((None, None, SSD_R, SSD_P, SSD_N), lambda b, g: (b, layer, g, 0, 0))
        in_specs += [st_spec, st_spec]
        args += [h0[0], h0[1]]
    aliases = {}
    if dst is not None:
        in_specs.append(pl.BlockSpec(memory_space=pl.ANY))
        args.append(dst)
        aliases = {len(args) - 1: 0}
    out_specs = [pl.BlockSpec((seq, 512), lambda b, g: (rb + b, g))]
    out_shape = [jax.ShapeDtypeStruct((proj.shape[0], D), F32)]
    if out_state:
        so = pl.BlockSpec((None, SSD_R, SSD_P, SSD_N), lambda b, g: (b, g, 0, 0))
        out_specs += [so, so]
        out_shape += [jax.ShapeDtypeStruct((nb, SSD_H, SSD_P, SSD_N), F32)] * 2
    return pl.pallas_call(
        functools.partial(_ssd_kernel, seq=seq, has_h0=has_h0, out_state=out_state, has_dst=dst is not None),
        input_output_aliases=aliases,
        grid=(nb, SSD_G),
        in_specs=in_specs,
        out_specs=out_specs,
        out_shape=out_shape,
        scratch_shapes=[pltpu.VMEM((seq, 512), F32), pltpu.VMEM((seq, 128), F32),
                        pltpu.VMEM((seq, 128), F32), pltpu.VMEM((2, SSD_R // 2, 128, 128), F32)],
        compiler_params=_cp(("parallel", "parallel")),
        name="ssd_latent" if has_h0 else "ssd_ctx",
    )(*args)


def _att_kernel(*refs, seq, past, tq, latent, has_dst):
    it = iter(refs)
    q_ref, k_ref, v_ref, qg_ref, kg_ref = next(it), next(it), next(it), next(it), next(it)
    if latent:
        kc_ref, vc_ref, cos_ref, sa_ref, sb_ref = next(it), next(it), next(it), next(it), next(it)
    if has_dst:
        next(it)
    o_ref = next(it)
    if not latent:
        ko_ref, vo_ref = next(it), next(it)
    kb_ref, vb_ref = next(it), next(it)
    qi = pl.program_id(2)

    @pl.when(qi == 0)
    def _():
        kn = _rms(k_ref[...], kg_ref[...])
        v = v_ref[...]
        if latent:
            kn = _rope(kn, cos_ref[...], sa_ref[...], sb_ref[...])
            kb_ref[0:past, :] = kc_ref[...].astype(BF16)
            vb_ref[0:past, 0:ATT_D] = vc_ref[...].astype(BF16)
            kb_ref[past:past + seq, :] = kn.astype(BF16)
            vb_ref[past:past + seq, 0:ATT_D] = v.astype(BF16)
        else:
            ko_ref[...] = kn
            vo_ref[...] = v
            kb_ref[...] = kn.astype(BF16)
            vb_ref[:, 0:ATT_D] = v.astype(BF16)
        vb_ref[:, ATT_D:2 * ATT_D] = jnp.ones((past + seq, ATT_D), BF16)

    scale = ATT_D ** -0.5 * 1.4426950408889634
    kb = kb_ref[...]
    vb = vb_ref[...]
    if latent:
        q0 = pl.multiple_of(qi * tq, tq)
        cos = cos_ref[pl.ds(q0, tq), :]
        sa = sa_ref[pl.ds(q0, tq), :]
        sb = sb_ref[pl.ds(q0, tq), :]
    for r in range(ATT_R):
        qn = _rms(q_ref[:, r * ATT_D:(r + 1) * ATT_D], qg_ref[...])
        if latent:
            qn = _rope(qn, cos, sa, sb)
        s = _dot_nt((qn * scale).astype(BF16), kb)
        m = jnp.max(s, axis=-1, keepdims=True)
        p = jnp.exp2(s - m)
        ol = _dot(p.astype(BF16), vb)
        o = ol[:, 0:ATT_D] / ol[:, ATT_D:ATT_D + 1]
        o_ref[:, r * ATT_D:(r + 1) * ATT_D] = o.astype(BF16)


def _att(proj, row_off, nb, seq, qg, kg, layer, tq, ctx=None, rope=None, dst=None):
    latent = ctx is not None
    past = ctx[0].shape[3] if latent else 0
    nq = seq // tq
    rbq = row_off // tq
    rbs = row_off // seq
    in_specs = [
        pl.BlockSpec((tq, 512), lambda b, g, q: (rbq + b * nq + q, OFF_AQ // 512 + g)),
        pl.BlockSpec((seq, 128), lambda b, g, q: (rbs + b, OFF_AK // 128 + g)),
        pl.BlockSpec((seq, 128), lambda b, g, q: (rbs + b, OFF_AV // 128 + g)),
        pl.BlockSpec((None, 1, ATT_D), lambda b, g, q: (layer, 0, 0)),
        pl.BlockSpec((None, 1, ATT_D), lambda b, g, q: (layer, 0, 0)),
    ]
    args = [proj, proj, proj, qg, kg]
    if latent:
        cs = pl.BlockSpec((None, None, None, past, ATT_D), lambda b, g, q: (b, layer, g, 0, 0))
        ts = pl.BlockSpec((seq, ATT_D), lambda b, g, q: (0, 0))
        in_specs += [cs, cs, ts, ts, ts]
        args += [ctx[0], ctx[1], rope[0], rope[1], rope[2]]
    aliases = {}
    if dst is not None:
        in_specs.append(pl.BlockSpec(memory_space=pl.ANY))
        args.append(dst)
        aliases = {len(args) - 1: 0}
    out_specs = [pl.BlockSpec((tq, 512), lambda b, g, q: (rbq + b * nq + q, g))]
    out_shape = [jax.ShapeDtypeStruct((proj.shape[0], D), BF16)]
    if not latent:
        so = pl.BlockSpec((None, None, seq, ATT_D), lambda b, g, q: (b, g, 0, 0))
        out_specs += [so, so]
        out_shape += [jax.ShapeDtypeStruct((nb, ATT_G, seq, ATT_D), F32)] * 2
    return pl.pallas_call(
        functools.partial(_att_kernel, seq=seq, past=past, tq=tq, latent=latent, has_dst=dst is not None),
        input_output_aliases=aliases,
        grid=(nb, ATT_G, nq),
        in_specs=in_specs,
        out_specs=out_specs,
        out_shape=out_shape,
        scratch_shapes=[pltpu.VMEM((past + seq, ATT_D), BF16), pltpu.VMEM((past + seq, 2 * ATT_D), BF16)],
        compiler_params=_cp(("parallel", "parallel", "arbitrary")),
        name="att_latent" if latent else "att_ctx",
    )(*args)


def _ret_kernel(*refs, seq, latent, out_state, has_dst):
    it = iter(refs)
    q_ref, k_ref, v_ref, rg_ref, rd_ref, ng_ref = (next(it) for _ in range(6))
    if latent:
        cos_ref, sa_ref, sb_ref, s0f_ref, s0b_ref = (next(it) for _ in range(5))
    if has_dst:
        next(it)
    o_ref = next(it)
    if out_state:
        sout_refs = (next(it), next(it))
    of_ref, st_ref = next(it), next(it)

    nc = seq // CH
    ri = lax.broadcasted_iota(jnp.int32, (CH, CH), 0)
    ci = lax.broadcasted_iota(jnp.int32, (CH, CH), 1)
    pos = ri.astype(F32)
    diff = (ri - ci).astype(F32)
    scale = RET_K ** -0.5
    rd = rd_ref[...]
    lg_all = jnp.minimum(rd, 0.0) - jnp.log(1.0 + jnp.exp(-jnp.abs(rd)))

    for d in range(2):
        lg = lg_all[d:d + 1, :]
        if d == 0:
            dm = jnp.exp(jnp.where(ri >= ci, diff * lg, -jnp.inf))
            kte = jnp.exp((CH - 1.0 - pos) * lg)
            qfs = jnp.exp((pos + 1.0) * lg)
        else:
            dm = jnp.exp(jnp.where(ci > ri, -diff * lg, -jnp.inf))
            kte = jnp.exp(pos * lg)
            qfs = jnp.exp((CH - pos) * lg)
        cdec = jnp.exp(CH * lg)
        cdec2 = jnp.concatenate([cdec, cdec], axis=1)
        if latent:
            st_ref[...] = (s0f_ref if d == 0 else s0b_ref)[...]
        else:
            st_ref[...] = jnp.zeros_like(st_ref)

        def chunk(c, carry, d=d, dm=dm, kte=kte, qfs=qfs, cdec2=cdec2):
            ch = c if d == 0 else nc - 1 - c
            t0 = pl.multiple_of(ch * CH, CH)
            q = q_ref[pl.ds(t0, CH), :] * scale
            k = k_ref[pl.ds(t0, CH), :]
            if latent:
                cos = cos_ref[pl.ds(t0, CH), :]
                sa = sa_ref[pl.ds(t0, CH), :]
                sb = sb_ref[pl.ds(t0, CH), :]
                q = _rope(q, cos, sa, sb)
                k = _rope(k, cos, sa, sb)
            vb = v_ref[pl.ds(t0, CH), :].astype(BF16)
            sc = _dot_nt(q.astype(BF16), k.astype(BF16)) * dm
            st = st_ref[...]
            o = _dot(sc.astype(BF16), vb) + _dot((q * qfs).astype(BF16), st.astype(BF16))
            st_ref[...] = cdec2 * st + _dot((k * kte).T.astype(BF16), vb)
            if d == 0:
                of_ref[pl.ds(t0, CH), :] = o
            else:
                tot = of_ref[pl.ds(t0, CH), :] + o
                y = _silu(rg_ref[pl.ds(t0, CH), :]) * _rms(tot, ng_ref[...])
                o_ref[pl.ds(t0, CH), :] = y.astype(BF16)
            return carry

        lax.fori_loop(0, nc, chunk, 0, unroll=min(nc, 4))
        if out_state:
            sout_refs[d][...] = st_ref[...]


def _ret(proj, row_off, nb, seq, rd, ng, layer, rope=None, s0=None, out_state=False, dst=None):
    latent = s0 is not None
    rb = row_off // seq
    in_specs = [
        pl.BlockSpec((seq, RET_K), lambda b, h: (rb + b, OFF_RQ // RET_K + h)),
        pl.BlockSpec((seq, RET_K), lambda b, h: (rb + b, OFF_RK // RET_K + h)),
        pl.BlockSpec((seq, RET_V), lambda b, h: (rb + b, OFF_RV // RET_V + h)),
        pl.BlockSpec((seq, RET_V), lambda b, h: (rb + b, OFF_RG // RET_V + h)),
        pl.BlockSpec((None, None, 2, 128), lambda b, h: (layer, h, 0, 0)),
        pl.BlockSpec((None, 1, RET_V), lambda b, h: (layer, 0, h)),
    ]
    args = [proj, proj, proj, proj, rd, ng]
    if latent:
        ts = pl.BlockSpec((seq, RET_K), lambda b, h: (0, 0))
        ss = pl.BlockSpec((None, None, None, RET_K, RET_V), lambda b, h: (b, layer, h, 0, 0))
        in_specs += [ts, ts, ts, ss, ss]
        args += [rope[0], rope[1], rope[2], s0[0], s0[1]]
    aliases = {}
    if dst is not None:
        in_specs.append(pl.BlockSpec(memory_space=pl.ANY))
        args.append(dst)
        aliases = {len(args) - 1: 0}
    out_specs = [pl.BlockSpec((seq, RET_V), lambda b, h: (rb + b, h))]
    out_shape = [jax.ShapeDtypeStruct((proj.shape[0], D), BF16)]
    if out_state:
        so = pl.BlockSpec((None, None, RET_K, RET_V), lambda b, h: (b, h, 0, 0))
        out_specs += [so, so]
        out_shape += [jax.ShapeDtypeStruct((nb, RET_H, RET_K, RET_V), F32)] * 2
    return pl.pallas_call(
        functools.partial(_ret_kernel, seq=seq, latent=latent, out_state=out_state, has_dst=dst is not None),
        input_output_aliases=aliases,
        grid=(nb, RET_H),
        in_specs=in_specs,
        out_specs=out_specs,
        out_shape=out_shape,
        scratch_shapes=[pltpu.VMEM((seq, RET_V), F32), pltpu.VMEM((RET_K, RET_V), F32)],
        compiler_params=_cp(("parallel", "parallel")),
        name="ret_latent" if latent else "ret_ctx",
    )(*args)


def _merge_kernel(ys_ref, z_ref, ng_ref, ya_ref, yr_ref, g0_ref, g1_ref, g2_ref, wb_ref, o_ref, y0_ref):
    @pl.when(pl.program_id(1) == 0)
    def _():
        y = ys_ref[...] * _silu(z_ref[...])
        y0_ref[...] = _rms(y, ng_ref[...]).astype(BF16)

    acc = _sigmoid(g0_ref[...]) * _dot(y0_ref[...], wb_ref[0])
    acc += _sigmoid(g1_ref[...]) * _dot(ya_ref[...], wb_ref[1])
    acc += _sigmoid(g2_ref[...]) * _dot(yr_ref[...], wb_ref[2])
    o_ref[...] = acc.astype(BF16)


def _merge(y_ssd, proj, ssd_ng, y_att, y_ret, w_branch, layer, tm=512, tn=512):
    t = y_ssd.shape[0]
    gl = OFF_GL // tn
    nj = D // tn
    return pl.pallas_call(
        _merge_kernel,
        grid=(t // tm, nj),
        in_specs=[pl.BlockSpec((tm, D), lambda i, j: (i, 0)),
                  pl.BlockSpec((tm, D), lambda i, j: (i, OFF_Z // D)),
                  pl.BlockSpec((None, 1, D), lambda i, j: (layer, 0, 0)),
                  pl.BlockSpec((tm, D), lambda i, j: (i, 0)),
                  pl.BlockSpec((tm, D), lambda i, j: (i, 0)),
                  pl.BlockSpec((tm, tn), lambda i, j: (i, gl + j)),
                  pl.BlockSpec((tm, tn), lambda i, j: (i, gl + nj + j)),
                  pl.BlockSpec((tm, tn), lambda i, j: (i, gl + 2 * nj + j)),
                  pl.BlockSpec((None, 3, D, tn), lambda i, j: (layer, 0, 0, j))],
        out_specs=pl.BlockSpec((tm, tn), lambda i, j: (i, j)),
        out_shape=jax.ShapeDtypeStruct((t, D), BF16),
        scratch_shapes=[pltpu.VMEM((tm, D), BF16)],
        compiler_params=_cp(("parallel", "arbitrary")),
        name="merge",
    )(y_ssd, proj, ssd_ng, y_att, y_ret, proj, proj, proj, w_branch)


def _outproj_kernel(m_ref, w_ref, x_ref, g1_ref, ng_ref, sh_ref, sc_ref, wr_ref,
                    x1_ref, h2_ref, aff_ref, afft_ref):
    x1 = x_ref[...] + g1_ref[...] * _dot(m_ref[...], w_ref[...])
    x1_ref[...] = x1
    h = _rms(x1, ng_ref[...]) * (1.0 + sc_ref[...]) + sh_ref[...]
    hb = h.astype(BF16)
    h2_ref[...] = hb
    hl = (h - hb.astype(F32)).astype(BF16)
    logits = _dot(hb, wr_ref[0]) + _dot(hl, wr_ref[0]) + _dot(hb, wr_ref[1])
    lane = lax.broadcasted_iota(jnp.int32, logits.shape, 1)
    logits = jnp.where(lane < N_EXP, logits, -jnp.inf)
    e = jnp.exp(logits - jnp.max(logits, axis=-1, keepdims=True))
    aff = e / jnp.sum(e, axis=-1, keepdims=True)
    aff_ref[...] = aff
    afft_ref[...] = aff.T[0:N_EXP, :]


def _outproj(merged, w_out, x, mods, norm_g, w_router, layer, mod_row, tm=512):
    t = x.shape[0]

    def ms(col):
        return pl.BlockSpec((None, None, 1, D), lambda i: (layer, mod_row(i, tm), 0, col))

    return pl.pallas_call(
        _outproj_kernel,
        grid=(t // tm,),
        in_specs=[pl.BlockSpec((tm, D), lambda i: (i, 0)),
                  pl.BlockSpec((None, D, D), lambda i: (layer, 0, 0)),
                  pl.BlockSpec((tm, D), lambda i: (i, 0)),
                  ms(2),
                  pl.BlockSpec((None, 1, D), lambda i: (layer, 0, 0)),
                  ms(3), ms(4),
                  pl.BlockSpec((None, 2, D, 128), lambda i: (layer, 0, 0, 0))],
        out_specs=[pl.BlockSpec((tm, D), lambda i: (i, 0)),
                   pl.BlockSpec((tm, D), lambda i: (i, 0)),
                   pl.BlockSpec((tm, 128), lambda i: (i, 0)),
                   pl.BlockSpec((N_EXP, tm), lambda i: (0, i))],
        out_shape=[jax.ShapeDtypeStruct((t, D), F32),
                   jax.ShapeDtypeStruct((t, D), BF16),
                   jax.ShapeDtypeStruct((t, 128), F32),
                   jax.ShapeDtypeStruct((N_EXP, t), F32)],
        compiler_params=_cp(("parallel",)),
        name="out_proj",
    )(merged, w_out, x, mods, norm_g, mods, mods, w_router)


SUB = 128
WIN = SUB + 16
SWIN = 48


def _select_kernel(a_ref, post_ref, postok_ref, st_ref, cnt_ref, *, groups):
    ri = lax.broadcasted_iota(jnp.int32, (SUB, SUB), 0)
    ci = lax.broadcasted_iota(jnp.int32, (SUB, SUB), 1)
    before = jnp.where(ri < ci, 1.0, 0.0).astype(BF16)
    lane = lax.broadcasted_iota(jnp.int32, (N_EXP, SUB), 1)
    st_ref[...] = jnp.zeros_like(st_ref)
    for lo, n, cap, base in groups:
        bits = pltpu.bitcast(a_ref[:, lo:lo + n], jnp.int32)

        def search(i, prefix, bits=bits, cap=cap):
            cand = prefix | jnp.left_shift(jnp.int32(1), 30 - i)
            cnt = jnp.sum(jnp.where(bits >= cand, 1.0, 0.0), axis=1, keepdims=True)
            return jnp.where(cnt >= cap, cand, prefix)

        zero = pltpu.bitcast(jnp.minimum(jnp.min(a_ref[:, lo:lo + n], axis=1, keepdims=True), 0.0), jnp.int32)
        tau = lax.fori_loop(0, 31, search, zero)
        need = cap - jnp.sum(jnp.where(bits > tau, 1.0, 0.0), axis=1, keepdims=True)

        cnt_ref[...] = jnp.zeros_like(cnt_ref)

        def block(b, carry, lo=lo, base=base, tau=tau, need=need):
            ceq = cnt_ref[0][:, 0:1]
            csel = cnt_ref[1][:, 0:1]
            t0 = pl.multiple_of(lo + b * SUB, SUB)
            bb = pltpu.bitcast(a_ref[:, pl.ds(t0, SUB)], jnp.int32)
            eq = jnp.where(bb == tau, 1.0, 0.0)
            rank_eq = ceq + _dot(eq.astype(BF16), before)
            sel = jnp.where((bb > tau) | ((bb == tau) & (rank_eq < need)), 1.0, 0.0)
            pos = csel + _dot(sel.astype(BF16), before) + base
            pm = jnp.where(sel > 0.0, pos, -1.0)
            post_ref[:, pl.ds(t0, SUB)] = pm
            full = jnp.concatenate([pm, jnp.full((SUB - N_EXP, SUB), -1.0, F32)], axis=0)
            postok_ref[pl.ds(t0, SUB), :] = full.T
            st_ref[...] = jnp.where(lane == lo // SUB + b, (csel + base).astype(jnp.int32), st_ref[...])
            cnt_ref[0] = jnp.broadcast_to(ceq + jnp.sum(eq, axis=1, keepdims=True), (N_EXP, SUB))
            cnt_ref[1] = jnp.broadcast_to(csel + jnp.sum(sel, axis=1, keepdims=True), (N_EXP, SUB))
            return carry

        lax.fori_loop(0, n // SUB, block, 0)
    lo, n, cap, base = groups[-1]
    st_ref[...] = jnp.where(lane == (lo + n) // SUB, base + cap, st_ref[...])


def _select(afft, groups):
    t = afft.shape[1]
    assert t // SUB < SUB
    return pl.pallas_call(
        functools.partial(_select_kernel, groups=groups),
        out_shape=[jax.ShapeDtypeStruct((N_EXP, t), F32),
                   jax.ShapeDtypeStruct((t, SUB), F32),
                   jax.ShapeDtypeStruct((N_EXP, SUB), jnp.int32)],
        scratch_shapes=[pltpu.VMEM((2, N_EXP, SUB), F32)],
        compiler_params=pltpu.CompilerParams(vmem_limit_bytes=VMEM_LIMIT),
        name="select",
    )(afft)


def _ffn_kernel(st_ref, h_ref, pos_ref, w1_ref, w3_ref, w2_ref, o_ref, stage_ref, hid_ref,
                *, n_e, nbd, nf, n2, cap, tbd, tf):
    r = pl.program_id(0)
    s = pl.program_id(1)

    @pl.when((r < n_e) & (s < nbd))
    def _():
        slot = r % 2

        @pl.when(s == 0)
        def _():
            stage_ref[slot] = jnp.zeros(stage_ref.shape[1:], BF16)

        srow = lax.broadcasted_iota(jnp.int32, (WIN, SUB), 0).astype(F32)
        for k in range(tbd // SUB):
            sub = s * (tbd // SUB) + k
            end = st_ref[r, sub + 1]
            off16 = pl.multiple_of((st_ref[r, sub] // 16) * 16, 16)
            rel = pos_ref[:, k * SUB:(k + 1) * SUB] - off16.astype(F32)

            def place(rows, k=k, off16=off16, rel=rel):
                onehot = jnp.where(srow[0:rows, :] == rel, 1.0, 0.0).astype(BF16)
                got = _dot(onehot, h_ref[k * SUB:(k + 1) * SUB, :]).astype(BF16)
                stage_ref[slot, pl.ds(off16, rows), :] += got

            pl.when(end - off16 <= SWIN)(functools.partial(place, SWIN))
            pl.when(end - off16 > SWIN)(functools.partial(place, WIN))

    @pl.when((r > 0) & (s < nf))
    def _():
        x = stage_ref[(r + 1) % 2, 0:cap, :]
        hid = _silu(_dot(x, w1_ref[...].astype(BF16))) * _dot(x, w3_ref[...].astype(BF16))
        hid_ref[:, pl.ds(pl.multiple_of(s * tf, tf), tf)] = hid.astype(BF16)

    @pl.when((r > 0) & (s >= nf) & (s < nf + n2))
    def _():
        o_ref[...] = _dot(hid_ref[...], w2_ref[...].astype(BF16)).astype(BF16)


def _ffn(starts, h2, post, w1, w3, w2, layer, cap, tf=256, tn=256):
    t = h2.shape[0]
    n_e, ff = w1.shape[1], w1.shape[-1]
    nf = ff // tf
    n2 = D // tn
    tbd = max(SUB, (t // (nf + n2)) // SUB * SUB)
    while t % tbd:
        tbd -= SUB
    nbd = t // tbd
    steps = max(nbd, nf + n2)

    def tok(r, s):
        return jnp.where(r < n_e, jnp.minimum(s, nbd - 1), nbd - 1)

    def prev(r):
        return jnp.maximum(r - 1, 0)

    def fcol(s):
        return jnp.minimum(s, nf - 1)

    def ocol(r, s):
        return jnp.where(r > 0, jnp.clip(s - nf, 0, n2 - 1), 0)

    grid_spec = pltpu.PrefetchScalarGridSpec(
        num_scalar_prefetch=1,
        grid=(n_e + 1, steps),
        in_specs=[pl.BlockSpec((tbd, D), lambda r, s, st: (tok(r, s), 0)),
                  pl.BlockSpec((None, 1, tbd), lambda r, s, st: (jnp.minimum(r, n_e - 1), 0, tok(r, s))),
                  pl.BlockSpec((None, None, D, tf), lambda r, s, st: (layer, prev(r), 0, fcol(s))),
                  pl.BlockSpec((None, None, D, tf), lambda r, s, st: (layer, prev(r), 0, fcol(s))),
                  pl.BlockSpec((None, None, ff, tn), lambda r, s, st: (layer, prev(r), 0, ocol(r, s)))],
        out_specs=pl.BlockSpec((None, cap, tn), lambda r, s, st: (prev(r), 0, ocol(r, s))),
        scratch_shapes=[pltpu.VMEM((2, cap + WIN, D), BF16), pltpu.VMEM((cap, ff), BF16)],
    )
    return pl.pallas_call(
        functools.partial(_ffn_kernel, n_e=n_e, nbd=nbd, nf=nf, n2=n2, cap=cap, tbd=tbd, tf=tf),
        grid_spec=grid_spec,
        out_shape=jax.ShapeDtypeStruct((n_e, cap, D), BF16),
        compiler_params=_cp(("arbitrary", "arbitrary")),
        name="expert_ffn",
    )(starts, h2, post.reshape(n_e, 1, t), w1, w3, w2)


def _combine_kernel(st_ref, ye_hbm, postok_ref, aff_ref, x1_ref, g2_ref, o_ref, win_ref, sem, *, cap):
    b = pl.program_id(0)
    nb = pl.num_programs(0)

    def first_slot(bb, e):
        return pl.multiple_of(jnp.minimum((st_ref[e, bb] // 16) * 16, cap - WIN), 16)

    def window_copy(bb, slot, e):
        return pltpu.make_async_copy(ye_hbm.at[e, pl.ds(first_slot(bb, e), WIN), :],
                                     win_ref.at[slot, pl.ds(e * WIN, WIN), :], sem.at[slot])

    @pl.when(b == 0)
    def _():
        tail = jnp.zeros((2 * SUB - WIN, D), BF16)
        win_ref[0, N_EXP * WIN:, :] = tail
        win_ref[1, N_EXP * WIN:, :] = tail
        for e in range(N_EXP):
            window_copy(0, 0, e).start()

    @pl.when(b + 1 < nb)
    def _():
        for e in range(N_EXP):
            window_copy(b + 1, (b + 1) % 2, e).start()

    slot = b % 2
    for e in range(N_EXP):
        window_copy(b, slot, e).wait()

    lane = lax.broadcasted_iota(jnp.int32, (SUB, 2 * SUB), 1).astype(F32)
    acc = jnp.zeros((SUB, D), F32)
    for e in range(N_EXP):
        rel = postok_ref[:, e:e + 1] - first_slot(b, e).astype(F32)
        gated = jnp.where(lane == rel, aff_ref[:, e:e + 1], 0.0).astype(BF16)
        acc += _dot(gated, win_ref[slot, pl.ds(e * WIN, 2 * SUB), :])
    o_ref[...] = x1_ref[...] + g2_ref[...] * acc


def _combine(starts, ye, postok, aff, x1, mods, layer, mod_row):
    t = x1.shape[0]
    n_e, cap, _ = ye.shape
    grid_spec = pltpu.PrefetchScalarGridSpec(
        num_scalar_prefetch=1,
        grid=(t // SUB,),
        in_specs=[pl.BlockSpec(memory_space=pl.ANY),
                  pl.BlockSpec((SUB, SUB), lambda b, st: (b, 0)),
                  pl.BlockSpec((SUB, SUB), lambda b, st: (b, 0)),
                  pl.BlockSpec((SUB, D), lambda b, st: (b, 0)),
                  pl.BlockSpec((None, None, 1, D), lambda b, st: (layer, mod_row(b, SUB), 0, 5))],
        out_specs=pl.BlockSpec((SUB, D), lambda b, st: (b, 0)),
        scratch_shapes=[pltpu.VMEM((2, N_EXP * WIN + 2 * SUB - WIN, D), BF16),
                        pltpu.SemaphoreType.DMA((2,))],
    )
    return pl.pallas_call(
        functools.partial(_combine_kernel, cap=cap),
        grid_spec=grid_spec,
        out_shape=jax.ShapeDtypeStruct((t, D), F32),
        compiler_params=_cp(("arbitrary",)),
        name="combine",
    )(starts, ye, postok, aff, x1, mods)


def _rope_tables(n_tokens):
    rows_n = n_tokens // GRID_W
    row = jnp.repeat(jnp.arange(rows_n, dtype=F32), GRID_W)
    col = jnp.tile(jnp.arange(GRID_W, dtype=F32), rows_n)
    n_freq = ATT_D // 4
    inv = ROPE_THETA ** (-jnp.arange(n_freq, dtype=F32) / n_freq)
    ang_r = row[:, None] * inv
    ang_c = col[:, None] * inv
    zeros = jnp.zeros_like(ang_r)
    cos = jnp.concatenate([jnp.cos(ang_r)] * 2 + [jnp.cos(ang_c)] * 2, axis=1)
    sa = jnp.concatenate([-jnp.sin(ang_r), zeros, -jnp.sin(ang_c), zeros], axis=1)
    sb = jnp.concatenate([zeros, jnp.sin(ang_r), zeros, jnp.sin(ang_c)], axis=1)
    return cos, sa, sb


def _group_dt_lanes(v):
    depth = v.shape[0]
    v = v.reshape(depth, 2, SSD_G, SSD_R).transpose(0, 2, 1, 3).reshape(depth, SSD_G, 1, 2 * SSD_R)
    return jnp.pad(v, ((0, 0), (0, 0), (0, 0), (0, 128 - 2 * SSD_R)))


PREP_TN = 512
SHIFT = ORIG_AFTER_DT - ORIG_DT


def _prep_kernel(a_ref, b_ref, dt_ref, o_ref):
    j = pl.program_id(1)
    first_shifted = ORIG_DT // PREP_TN
    last = pl.num_programs(1) - 1

    @pl.when(j < first_shifted)
    def _():
        o_ref[...] = a_ref[...].astype(BF16)

    @pl.when((j >= first_shifted) & (j < last))
    def _():
        a = pltpu.roll(a_ref[...], PREP_TN - SHIFT, 1)
        b = pltpu.roll(b_ref[...], 128 - SHIFT, 1)
        lane = lax.broadcasted_iota(jnp.int32, b.shape, 1)
        o_ref[:, 0:PREP_TN - 128] = a[:, 0:PREP_TN - 128].astype(BF16)
        o_ref[:, PREP_TN - 128:] = jnp.where(lane < 128 - SHIFT, a[:, PREP_TN - 128:], b).astype(BF16)

    @pl.when(j == last)
    def _():
        o_ref[...] = dt_ref[...]


def _prep_w_in(w_in):
    depth = w_in.shape[0]
    dt = w_in[:, :, ORIG_DT:ORIG_AFTER_DT].reshape(depth, D, 2, SSD_G, SSD_R)
    dt = dt.transpose(0, 1, 3, 2, 4).reshape(depth, D, SSD_G, 2 * SSD_R)
    dt = jnp.pad(dt, ((0, 0), (0, 0), (0, 0), (0, 128 - 2 * SSD_R))).reshape(depth, D, SSD_G * 128)
    nj = NW // PREP_TN
    assert ORIG_DT % PREP_TN == 0 and SSD_G * 128 == PREP_TN and SHIFT < 128

    def src(j):
        return jnp.minimum(j, nj - 2)

    return pl.pallas_call(
        _prep_kernel,
        grid=(depth, nj),
        in_specs=[pl.BlockSpec((None, D, PREP_TN), lambda l, j: (l, 0, src(j))),
                  pl.BlockSpec((None, D, 128), lambda l, j: (l, 0, (src(j) + 1) * (PREP_TN // 128))),
                  pl.BlockSpec((None, D, PREP_TN), lambda l, j: (l, 0, 0))],
        out_specs=pl.BlockSpec((None, D, PREP_TN), lambda l, j: (l, 0, j)),
        out_shape=jax.ShapeDtypeStruct((depth, D, NW), BF16),
        compiler_params=_cp(("parallel", "parallel")),
        name="w_in_prep",
    )(w_in, w_in, dt.astype(BF16))


def kernel(x_prompt, x_sample, cache_attn_k, cache_attn_v, state_ssd_fwd, state_ssd_bwd, state_ret_fwd, state_ret_bwd, c, c_ctx, norm1_g, norm2_g, w_ada, b_ada, w_in, ssd_conv_w, ssd_conv_b, ssd_a_log, ssd_dt_bias, ssd_d, ssd_norm_g, q_norm_g, k_norm_g, ret_decay, ret_norm_g, w_branch, w_out, w_router, w_e1, w_e3, w_e2):
    depth = w_in.shape[0]
    pb, pseq, _ = x_prompt.shape
    sb, sseq, _ = x_sample.shape
    n_p = pb * pseq
    n_s = sb * sseq
    assert n_p % 1024 == 0 and sseq % 1024 == 0 and 1 + sb <= 8

    def mod_row(i, tm):
        return jnp.where(i < n_p // tm, 0, 1 + (i * tm - n_p) // sseq)

    w_in_b = _prep_w_in(w_in)
    w_branch_b = w_branch.astype(BF16)
    w_out_b = w_out.astype(BF16)
    cap_p = CAP_FACTOR * n_p // N_EXP
    cap_s = CAP_FACTOR * n_s // N_EXP
    groups = ((0, n_p, cap_p, 0), (n_p, n_s, cap_s, cap_p))
    w_router_f = jnp.pad(w_router, ((0, 0), (0, 0), (0, 128 - N_EXP)))
    w_router_hi = w_router_f.astype(BF16)
    w_router_p = jnp.stack([w_router_hi, (w_router_f - w_router_hi.astype(F32)).astype(BF16)], axis=1)
    dtb = _group_dt_lanes(ssd_dt_bias)
    alog = _group_dt_lanes(ssd_a_log)
    dvec = jnp.repeat(ssd_d, SSD_P, axis=1).reshape(depth, 1, D)
    rd = jnp.broadcast_to(ret_decay.transpose(0, 2, 1)[..., None], (depth, RET_H, 2, 128))
    rope = _rope_tables(sseq)
    n1g = norm1_g.reshape(depth, 1, D)
    n2g = norm2_g.reshape(depth, 1, D)
    ssd_ng = ssd_norm_g.reshape(depth, 1, D)
    ret_ng = ret_norm_g.reshape(depth, 1, D)
    qg = q_norm_g.reshape(depth, 1, ATT_D)
    kg = k_norm_g.reshape(depth, 1, ATT_D)

    cond8 = jnp.zeros((8, D), F32).at[0].set(c_ctx).at[1:1 + sb].set(c)
    mods = _ada(cond8, w_ada, b_ada).reshape(depth, 8, 1, 6 * D)

    x = jnp.concatenate([x_prompt.reshape(n_p, D), x_sample.reshape(n_s, D)], axis=0)
    new_k, new_v, new_hf, new_hb, new_sf, new_sb = [], [], [], [], [], []
    y_ssd = jnp.zeros((n_p + n_s, D), F32)
    y_att = jnp.zeros((n_p + n_s, D), BF16)
    y_ret = jnp.zeros((n_p + n_s, D), BF16)
    for l in range(depth):
        proj = _inproj(x, n1g, mods, w_in_b, l, mod_row)

        ys_p, hf, hb = _ssd(proj, 0, pb, pseq, ssd_conv_w, ssd_conv_b.reshape(depth, 1, -1), dtb, alog, dvec,
                            l, out_state=True, dst=y_ssd)
        (y_ssd,) = _ssd(proj, n_p, sb, sseq, ssd_conv_w, ssd_conv_b.reshape(depth, 1, -1), dtb, alog, dvec,
                        l, h0=(state_ssd_fwd, state_ssd_bwd), dst=ys_p)
        ya_p, kn, vn = _att(proj, 0, pb, pseq, qg, kg, l, tq=pseq, dst=y_att)
        (y_att,) = _att(proj, n_p, sb, sseq, qg, kg, l, tq=256, ctx=(cache_attn_k, cache_attn_v), rope=rope,
                        dst=ya_p)
        yr_p, sf, sbk = _ret(proj, 0, pb, pseq, rd, ret_ng, l, out_state=True, dst=y_ret)
        (y_ret,) = _ret(proj, n_p, sb, sseq, rd, ret_ng, l, rope=rope, s0=(state_ret_fwd, state_ret_bwd),
                        dst=yr_p)
        new_k.append(kn), new_v.append(vn), new_hf.append(hf), new_hb.append(hb)
        new_sf.append(sf), new_sb.append(sbk)

        merged = _merge(y_ssd, proj, ssd_ng, y_att, y_ret, w_branch_b, l)
        x1, h2, aff, afft = _outproj(merged, w_out_b, x, mods, n2g, w_router_p, l, mod_row)
        post, postok, starts = _select(afft, groups)
        ye = _ffn(starts, h2, post, w_e1, w_e3, w_e2, l, cap_p + cap_s)
        x = _combine(starts, ye, postok, aff, x1, mods, l, mod_row)

    y_prompt = x[:n_p].reshape(pb, pseq, D)
    y_sample = x[n_p:].reshape(sb, sseq, D)
    return (y_prompt, y_sample, jnp.stack(new_k, axis=1), jnp.stack(new_v, axis=1),
            jnp.stack(new_hf, axis=1), jnp.stack(new_hb, axis=1),
            jnp.stack(new_sf, axis=1), jnp.stack(new_sb, axis=1))
```

```python
import functools

import jax
import jax.numpy as jnp
from jax import lax
from jax.experimental import pallas as pl
from jax.experimental.pallas import tpu as pltpu

F32 = jnp.float32
BF16 = jnp.bfloat16
EPS = 1e-6
ROPE_THETA = 10000.0
GRID_W = 64

D = 2048
CH = 128
SSD_P = 64
SSD_N = 128
SSD_G = 4
SSD_H = 32
SSD_R = SSD_H // SSD_G
ATT_D = 128
ATT_H = 16
ATT_G = 4
ATT_R = ATT_H // ATT_G
RET_H = 8
RET_K = 128
RET_V = 256
N_EXP = 16
CAP_FACTOR = 2

OFF_Z = 0
OFF_X = 2048
OFF_B = 4096
OFF_C = 4608
OFF_AQ = 5120
OFF_AK = 7168
OFF_AV = 7680
OFF_RQ = 8192
OFF_RK = 9216
OFF_RV = 10240
OFF_RG = 12288
OFF_GL = 14336
OFF_DT = 20480
NW = OFF_DT + SSD_G * 128
ORIG_DT = 5120
ORIG_AFTER_DT = 5184
ORIG_W = 20544

VMEM_LIMIT = 56 * 1024 * 1024


def _cp(sem):
    return pltpu.CompilerParams(dimension_semantics=sem, vmem_limit_bytes=VMEM_LIMIT)


def _sigmoid(x):
    return 1.0 / (1.0 + jnp.exp(-x))


def _silu(x):
    return x * _sigmoid(x)


def _softplus(x):
    return jnp.maximum(x, 0.0) + jnp.log(1.0 + jnp.exp(-jnp.abs(x)))


def _rms(x, g):
    ms = jnp.mean(x * x, axis=-1, keepdims=True)
    return x * lax.rsqrt(ms + EPS) * g


def _dot(a, b):
    return jnp.dot(a, b, preferred_element_type=F32)


def _dot_nt(a, b):
    return lax.dot_general(a, b, (((1,), (1,)), ((), ())), preferred_element_type=F32)


def _rope(x, cos, sa, sb):
    return x * cos + pltpu.roll(x, 96, 1) * sa + pltpu.roll(x, 32, 1) * sb


def _ada_kernel(c_ref, w_ref, b_ref, o_ref):
    s = _silu(c_ref[...]).astype(BF16)
    o_ref[...] = _dot(s, w_ref[...].astype(BF16)) + b_ref[...]


def _ada(cond8, w_ada, b_ada):
    depth, _, n6 = w_ada.shape
    tn = 1024
    return pl.pallas_call(
        _ada_kernel,
        grid=(depth, n6 // tn),
        in_specs=[pl.BlockSpec((8, D), lambda l, j: (0, 0)),
                  pl.BlockSpec((None, D, tn), lambda l, j: (l, 0, j)),
                  pl.BlockSpec((None, 1, tn), lambda l, j: (l, 0, j))],
        out_specs=pl.BlockSpec((None, 8, tn), lambda l, j: (l, 0, j)),
        out_shape=jax.ShapeDtypeStruct((depth, 8, n6), F32),
        compiler_params=_cp(("parallel", "parallel")),
        name="ada_mod",
    )(cond8, w_ada, b_ada.reshape(depth, 1, n6))


def _inproj_kernel(x_ref, g_ref, sh_ref, sc_ref, w_ref, o_ref, hn_ref):
    @pl.when(pl.program_id(1) == 0)
    def _():
        h = _rms(x_ref[...], g_ref[...])
        hn_ref[...] = (h * (1.0 + sc_ref[...]) + sh_ref[...]).astype(BF16)

    o_ref[...] = _dot(hn_ref[...], w_ref[...])


def _inproj(x, norm_g, mods, w_in, layer, mod_row, tm=1024, tn=512):
    t = x.shape[0]
    return pl.pallas_call(
        _inproj_kernel,
        grid=(t // tm, NW // tn),
        in_specs=[pl.BlockSpec((tm, D), lambda i, j: (i, 0)),
                  pl.BlockSpec((None, 1, D), lambda i, j: (layer, 0, 0)),
                  pl.BlockSpec((None, None, 1, D), lambda i, j: (layer, mod_row(i, tm), 0, 0)),
                  pl.BlockSpec((None, None, 1, D), lambda i, j: (layer, mod_row(i, tm), 0, 1)),
                  pl.BlockSpec((None, D, tn), lambda i, j: (layer, 0, j))],
        out_specs=pl.BlockSpec((tm, tn), lambda i, j: (i, j)),
        out_shape=jax.ShapeDtypeStruct((t, NW), F32),
        scratch_shapes=[pltpu.VMEM((tm, D), BF16)],
        compiler_params=_cp(("parallel", "arbitrary")),
        name="in_proj",
    )(x, norm_g, mods, mods, w_in)


def _ssd_kernel(*refs, seq, has_h0, out_state, has_dst):
    it = iter(refs)
    x_ref, b_ref, c_ref, dt_ref = next(it), next(it), next(it), next(it)
    cwx_ref, cwb_ref, cwc_ref = next(it), next(it), next(it)
    cbx_ref, cbb_ref, cbc_ref = next(it), next(it), next(it)
    dtb_ref, alog_ref, dvec_ref = next(it), next(it), next(it)
    h0_refs = (next(it), next(it)) if has_h0 else None
    if has_dst:
        next(it)
    y_ref = next(it)
    hout_refs = (next(it), next(it)) if out_state else None
    xs_ref, bs_ref, cs_ref, st_ref = next(it), next(it), next(it), next(it)

    nc = seq // CH
    row1 = lax.broadcasted_iota(jnp.int32, (CH, 1), 0)

    def conv_chunk(c, carry):
        t0 = pl.multiple_of(c * CH, CH)
        pidx = pl.multiple_of(jnp.maximum(t0 - 8, 0), 8)
        nidx = pl.multiple_of(jnp.minimum(t0 + CH, seq - 8), 8)

        def conv(src, w_ref, bias_ref, dst):
            cur = src[pl.ds(t0, CH), :]
            prev_row = jnp.where(c > 0, src[pl.ds(pidx, 8), :][7:8, :], 0.0)
            next_row = jnp.where(c < nc - 1, src[pl.ds(nidx, 8), :][0:1, :], 0.0)
            xm = jnp.where(row1 == 0, prev_row, pltpu.roll(cur, 1, 0))
            xp = jnp.where(row1 == CH - 1, next_row, pltpu.roll(cur, CH - 1, 0))
            w = w_ref[...]
            v = xm * w[0:1, :] + cur * w[1:2, :] + xp * w[2:3, :] + bias_ref[...]
            v = _silu(v)
            dst[pl.ds(t0, CH), :] = v
            return v

        xs = conv(x_ref, cwx_ref, cbx_ref, xs_ref)
        y_ref[pl.ds(t0, CH), :] = dvec_ref[...] * xs
        conv(b_ref, cwb_ref, cbb_ref, bs_ref)
        conv(c_ref, cwc_ref, cbc_ref, cs_ref)
        return carry

    lax.fori_loop(0, nc, conv_chunk, 0)

    ri = lax.broadcasted_iota(jnp.int32, (CH, CH), 0)
    ci = lax.broadcasted_iota(jnp.int32, (CH, CH), 1)
    tri = (ri >= ci).astype(F32)
    left = ci < SSD_P
    left_row = ci[0:1, :] < SSD_P
    dtb = dtb_ref[...]
    aneg = -jnp.exp(alog_ref[...])

    for d in range(2):
        for pair in range(SSD_R // 2):
            if has_h0:
                st_ref[d, pair] = h0_refs[d][2 * pair:2 * pair + 2].reshape(2 * SSD_P, SSD_N).T
            else:
                st_ref[d, pair] = jnp.zeros((SSD_N, 2 * SSD_P), F32)

    def scan_one(d, ch):
        mask = (ri >= ci) if d == 0 else (ci >= ri)
        t0 = pl.multiple_of(ch * CH, CH)
        dtv = _softplus(dt_ref[pl.ds(t0, CH), :] + dtb)
        a = dtv * aneg
        pref = jnp.dot(tri, a, precision=lax.Precision.HIGHEST, preferred_element_type=F32)
        tot = pref[CH - 1:CH, :]
        s = pref if d == 0 else tot - pref + a
        cd = jnp.exp(tot)
        s_t = s.T
        dt_t = dtv.T
        bc = bs_ref[pl.ds(t0, CH), :]
        bc_t = bc.T
        bcb = bc.astype(BF16)
        ccb = cs_ref[pl.ds(t0, CH), :].astype(BF16)
        cb = _dot_nt(ccb, bcb)
        for pair in range(SSD_R // 2):
            la = d * SSD_R + 2 * pair
            lb = la + 1
            x2 = xs_ref[pl.ds(t0, CH), pair * 128:(pair + 1) * 128]
            x_a = jnp.where(left, x2, 0.0).astype(BF16)
            x_b = jnp.where(left, 0.0, x2).astype(BF16)
            sa = jnp.broadcast_to(s[:, la:la + 1], (CH, CH))
            sb = jnp.broadcast_to(s[:, lb:lb + 1], (CH, CH))
            sa_t, sb_t = s_t[la:la + 1, :], s_t[lb:lb + 1, :]
            dta_t, dtb_t = dt_t[la:la + 1, :], dt_t[lb:lb + 1, :]
            w_a = (cb * jnp.exp(jnp.where(mask, sa - sa_t, -jnp.inf)) * dta_t).astype(BF16)
            w_b = (cb * jnp.exp(jnp.where(mask, sb - sb_t, -jnp.inf)) * dtb_t).astype(BF16)
            st = st_ref[d, pair]
            y = _dot(w_a, x_a) + _dot(w_b, x_b)
            y = y + _dot(ccb, st.astype(BF16)) * jnp.exp(jnp.where(left, sa, sb))
            g_a = jnp.exp(tot[:, la:la + 1] - sa_t) * dta_t
            g_b = jnp.exp(tot[:, lb:lb + 1] - sb_t) * dtb_t
            new = _dot((bc_t * g_a).astype(BF16), x_a) + _dot((bc_t * g_b).astype(BF16), x_b)
            cd2 = jnp.where(left_row, cd[:, la:la + 1], cd[:, lb:lb + 1])
            st_ref[d, pair] = cd2 * st + new
            y_ref[pl.ds(t0, CH), pair * 128:(pair + 1) * 128] += y

    def scan_chunk(c, carry):
        scan_one(0, c)
        scan_one(1, nc - 1 - c)
        return carry

    lax.fori_loop(0, nc, scan_chunk, 0, unroll=min(nc, 4))
    if out_state:
        for d in range(2):
            for pair in range(SSD_R // 2):
                hout_refs[d][2 * pair:2 * pair + 2] = st_ref[d, pair].T.reshape(2, SSD_P, SSD_N)


def _ssd(proj, row_off, nb, seq, conv_w, conv_b, dtb, alog, dvec, layer, h0=None, out_state=False, dst=None):
    rb = row_off // seq
    has_h0 = h0 is not None
    in_specs = [
        pl.BlockSpec((seq, 512), lambda b, g: (rb + b, OFF_X // 512 + g)),
        pl.BlockSpec((seq, 128), lambda b, g: (rb + b, OFF_B // 128 + g)),
        pl.BlockSpec((seq, 128), lambda b, g: (rb + b, OFF_C // 128 + g)),
        pl.BlockSpec((seq, 128), lambda b, g: (rb + b, OFF_DT // 128 + g)),
        pl.BlockSpec((None, 3, 512), lambda b, g: (layer, 0, g)),
        pl.BlockSpec((None, 3, 128), lambda b, g: (layer, 0, D // 128 + g)),
        pl.BlockSpec((None, 3, 128), lambda b, g: (layer, 0, D // 128 + SSD_G + g)),
        pl.BlockSpec((None, 1, 512), lambda b, g: (layer, 0, g)),
        pl.BlockSpec((None, 1, 128), lambda b, g: (layer, 0, D // 128 + g)),
        pl.BlockSpec((None, 1, 128), lambda b, g: (layer, 0, D // 128 + SSD_G + g)),
        pl.BlockSpec((None, None, 1, 128), lambda b, g: (layer, g, 0, 0)),
        pl.BlockSpec((None, None, 1, 128), lambda b, g: (layer, g, 0, 0)),
        pl.BlockSpec((None, 1, 512), lambda b, g: (layer, 0, g)),
    ]
    args = [proj, proj, proj, proj, conv_w, conv_w, conv_w, conv_b, conv_b, conv_b, dtb, alog, dvec]
    if has_h0:
        st_spec = pl.BlockSpec((None, None, SSD_R, SSD_P, SSD_N), lambda b, g: (b, layer, g, 0, 0))
        in_specs += [st_spec, st_spec]
        args += [h0[0], h0[1]]
    aliases = {}
    if dst is not None:
        in_specs.append(pl.BlockSpec(memory_space=pl.ANY))
        args.append(dst)
        aliases = {len(args) - 1: 0}
    out_specs = [pl.BlockSpec((seq, 512), lambda b, g: (rb + b, g))]
    out_shape = [jax.ShapeDtypeStruct((proj.shape[0], D), F32)]
    if out_state:
        so = pl.BlockSpec((None, SSD_R, SSD_P, SSD_N), lambda b, g: (b, g, 0, 0))
        out_specs += [so, so]
        out_shape += [jax.ShapeDtypeStruct((nb, SSD_H, SSD_P, SSD_N), F32)] * 2
    return pl.pallas_call(
        functools.partial(_ssd_kernel, seq=seq, has_h0=has_h0, out_state=out_state, has_dst=dst is not None),
        input_output_aliases=aliases,
        grid=(nb, SSD_G),
        in_specs=in_specs,
        out_specs=out_specs,
        out_shape=out_shape,
        scratch_shapes=[pltpu.VMEM((seq, 512), F32), pltpu.VMEM((seq, 128), F32),
                        pltpu.VMEM((seq, 128), F32), pltpu.VMEM((2, SSD_R // 2, 128, 128), F32)],
        compiler_params=_cp(("parallel", "parallel")),
        name="ssd_latent" if has_h0 else "ssd_ctx",
    )(*args)


def _att_kernel(*refs, seq, past, tq, latent, has_dst):
    it = iter(refs)
    q_ref, k_ref, v_ref, qg_ref, kg_ref = next(it), next(it), next(it), next(it), next(it)
    if latent:
        kc_ref, vc_ref, cos_ref, sa_ref, sb_ref = next(it), next(it), next(it), next(it), next(it)
    if has_dst:
        next(it)
    o_ref = next(it)
    if not latent:
        ko_ref, vo_ref = next(it), next(it)
    kb_ref, vb_ref = next(it), next(it)
    qi = pl.program_id(2)

    @pl.when(qi == 0)
    def _():
        kn = _rms(k_ref[...], kg_ref[...])
        v = v_ref[...]
        if latent:
            kn = _rope(kn, cos_ref[...], sa_ref[...], sb_ref[...])
            kb_ref[0:past, :] = kc_ref[...].astype(BF16)
            vb_ref[0:past, 0:ATT_D] = vc_ref[...].astype(BF16)
            kb_ref[past:past + seq, :] = kn.astype(BF16)
            vb_ref[past:past + seq, 0:ATT_D] = v.astype(BF16)
        else:
            ko_ref[...] = kn
            vo_ref[...] = v
            kb_ref[...] = kn.astype(BF16)
            vb_ref[:, 0:ATT_D] = v.astype(BF16)
        vb_ref[:, ATT_D:2 * ATT_D] = jnp.ones((past + seq, ATT_D), BF16)

    scale = ATT_D ** -0.5 * 1.4426950408889634
    kb = kb_ref[...]
    vb = vb_ref[...]
    if latent:
        q0 = pl.multiple_of(qi * tq, tq)
        cos = cos_ref[pl.ds(q0, tq), :]
        sa = sa_ref[pl.ds(q0, tq), :]
        sb = sb_ref[pl.ds(q0, tq), :]
    for r in range(ATT_R):
        qn = _rms(q_ref[:, r * ATT_D:(r + 1) * ATT_D], qg_ref[...])
        if latent:
            qn = _rope(qn, cos, sa, sb)
        s = _dot_nt((qn * scale).astype(BF16), kb)
        m = jnp.max(s, axis=-1, keepdims=True)
        p = jnp.exp2(s - m)
        ol = _dot(p.astype(BF16), vb)
        o = ol[:, 0:ATT_D] / ol[:, ATT_D:ATT_D + 1]
        o_ref[:, r * ATT_D:(r + 1) * ATT_D] = o.astype(BF16)


def _att(proj, row_off, nb, seq, qg, kg, layer, tq, ctx=None, rope=None, dst=None):
    latent = ctx is not None
    past = ctx[0].shape[3] if latent else 0
    nq = seq // tq
    rbq = row_off // tq
    rbs = row_off // seq
    in_specs = [
        pl.BlockSpec((tq, 512), lambda b, g, q: (rbq + b * nq + q, OFF_AQ // 512 + g)),
        pl.BlockSpec((seq, 128), lambda b, g, q: (rbs + b, OFF_AK // 128 + g)),
        pl.BlockSpec((seq, 128), lambda b, g, q: (rbs + b, OFF_AV // 128 + g)),
        pl.BlockSpec((None, 1, ATT_D), lambda b, g, q: (layer, 0, 0)),
        pl.BlockSpec((None, 1, ATT_D), lambda b, g, q: (layer, 0, 0)),
    ]
    args = [proj, proj, proj, qg, kg]
    if latent:
        cs = pl.BlockSpec((None, None, None, past, ATT_D), lambda b, g, q: (b, layer, g, 0, 0))
        ts = pl.BlockSpec((seq, ATT_D), lambda b, g, q: (0, 0))
        in_specs += [cs, cs, ts, ts, ts]
        args += [ctx[0], ctx[1], rope[0], rope[1], rope[2]]
    aliases = {}
    if dst is not None:
        in_specs.append(pl.BlockSpec(memory_space=pl.ANY))
        args.append(dst)
        aliases = {len(args) - 1: 0}
    out_specs = [pl.BlockSpec((tq, 512), lambda b, g, q: (rbq + b * nq + q, g))]
    out_shape = [jax.ShapeDtypeStruct((proj.shape[0], D), BF16)]
    if not latent:
        so = pl.BlockSpec((None, None, seq, ATT_D), lambda b, g, q: (b, g, 0, 0))
        out_specs += [so, so]
        out_shape += [jax.ShapeDtypeStruct((nb, ATT_G, seq, ATT_D), F32)] * 2
    return pl.pallas_call(
        functools.partial(_att_kernel, seq=seq, past=past, tq=tq, latent=latent, has_dst=dst is not None),
        input_output_aliases=aliases,
        grid=(nb, ATT_G, nq),
        in_specs=in_specs,
        out_specs=out_specs,
        out_shape=out_shape,
        scratch_shapes=[pltpu.VMEM((past + seq, ATT_D), BF16), pltpu.VMEM((past + seq, 2 * ATT_D), BF16)],
        compiler_params=_cp(("parallel", "parallel", "arbitrary")),
        name="att_latent" if latent else "att_ctx",
    )(*args)


def _ret_kernel(*refs, seq, latent, out_state, has_dst):
    it = iter(refs)
    q_ref, k_ref, v_ref, rg_ref, rd_ref, ng_ref = (next(it) for _ in range(6))
    if latent:
        cos_ref, sa_ref, sb_ref, s0f_ref, s0b_ref = (next(it) for _ in range(5))
    if has_dst:
        next(it)
    o_ref = next(it)
    if out_state:
        sout_refs = (next(it), next(it))
    of_ref, st_ref = next(it), next(it)

    nc = seq // CH
    ri = lax.broadcasted_iota(jnp.int32, (CH, CH), 0)
    ci = lax.broadcasted_iota(jnp.int32, (CH, CH), 1)
    pos = ri.astype(F32)
    diff = (ri - ci).astype(F32)
    scale = RET_K ** -0.5
    rd = rd_ref[...]
    lg_all = jnp.minimum(rd, 0.0) - jnp.log(1.0 + jnp.exp(-jnp.abs(rd)))

    for d in range(2):
        lg = lg_all[d:d + 1, :]
        if d == 0:
            dm = jnp.exp(jnp.where(ri >= ci, diff * lg, -jnp.inf))
            kte = jnp.exp((CH - 1.0 - pos) * lg)
            qfs = jnp.exp((pos + 1.0) * lg)
        else:
            dm = jnp.exp(jnp.where(ci > ri, -diff * lg, -jnp.inf))
            kte = jnp.exp(pos * lg)
            qfs = jnp.exp((CH - pos) * lg)
        cdec = jnp.exp(CH * lg)
        cdec2 = jnp.concatenate([cdec, cdec], axis=1)
        if latent:
            st_ref[...] = (s0f_ref if d == 0 else s0b_ref)[...]
        else:
            st_ref[...] = jnp.zeros_like(st_ref)

        def chunk(c, carry, d=d, dm=dm, kte=kte, qfs=qfs, cdec2=cdec2):
            ch = c if d == 0 else nc - 1 - c
            t0 = pl.multiple_of(ch * CH, CH)
            q = q_ref[pl.ds(t0, CH), :] * scale
            k = k_ref[pl.ds(t0, CH), :]
            if latent:
                cos = cos_ref[pl.ds(t0, CH), :]
                sa = sa_ref[pl.ds(t0, CH), :]
                sb = sb_ref[pl.ds(t0, CH), :]
                q = _rope(q, cos, sa, sb)
                k = _rope(k, cos, sa, sb)
            vb = v_ref[pl.ds(t0, CH), :].astype(BF16)
            sc = _dot_nt(q.astype(BF16), k.astype(BF16)) * dm
            st = st_ref[...]
            o = _dot(sc.astype(BF16), vb) + _dot((q * qfs).astype(BF16), st.astype(BF16))
            st_ref[...] = cdec2 * st + _dot((k * kte).T.astype(BF16), vb)
            if d == 0:
                of_ref[pl.ds(t0, CH), :] = o
            else:
                tot = of_ref[pl.ds(t0, CH), :] + o
                y = _silu(rg_ref[pl.ds(t0, CH), :]) * _rms(tot, ng_ref[...])
                o_ref[pl.ds(t0, CH), :] = y.astype(BF16)
            return carry

        lax.fori_loop(0, nc, chunk, 0, unroll=min(nc, 8))
        if out_state:
            sout_refs[d][...] = st_ref[...]


def _ret(proj, row_off, nb, seq, rd, ng, layer, rope=None, s0=None, out_state=False, dst=None):
    latent = s0 is not None
    rb = row_off // seq
    in_specs = [
        pl.BlockSpec((seq, RET_K), lambda b, h: (rb + b, OFF_RQ // RET_K + h)),
        pl.BlockSpec((seq, RET_K), lambda b, h: (rb + b, OFF_RK // RET_K + h)),
        pl.BlockSpec((seq, RET_V), lambda b, h: (rb + b, OFF_RV // RET_V + h)),
        pl.BlockSpec((seq, RET_V), lambda b, h: (rb + b, OFF_RG // RET_V + h)),
        pl.BlockSpec((None, None, 2, 128), lambda b, h: (layer, h, 0, 0)),
        pl.BlockSpec((None, 1, RET_V), lambda b, h: (layer, 0, h)),
    ]
    args = [proj, proj, proj, proj, rd, ng]
    if latent:
        ts = pl.BlockSpec((seq, RET_K), lambda b, h: (0, 0))
        ss = pl.BlockSpec((None, None, None, RET_K, RET_V), lambda b, h: (b, layer, h, 0, 0))
        in_specs += [ts, ts, ts, ss, ss]
        args += [rope[0], rope[1], rope[2], s0[0], s0[1]]
    aliases = {}
    if dst is not None:
        in_specs.append(pl.BlockSpec(memory_space=pl.ANY))
        args.append(dst)
        aliases = {len(args) - 1: 0}
    out_specs = [pl.BlockSpec((seq, RET_V), lambda b, h: (rb + b, h))]
    out_shape = [jax.ShapeDtypeStruct((proj.shape[0], D), BF16)]
    if out_state:
        so = pl.BlockSpec((None, None, RET_K, RET_V), lambda b, h: (b, h, 0, 0))
        out_specs += [so, so]
        out_shape += [jax.ShapeDtypeStruct((nb, RET_H, RET_K, RET_V), F32)] * 2
    return pl.pallas_call(
        functools.partial(_ret_kernel, seq=seq, latent=latent, out_state=out_state, has_dst=dst is not None),
        input_output_aliases=aliases,
        grid=(nb, RET_H),
        in_specs=in_specs,
        out_specs=out_specs,
        out_shape=out_shape,
        scratch_shapes=[pltpu.VMEM((seq, RET_V), F32), pltpu.VMEM((RET_K, RET_V), F32)],
        compiler_params=_cp(("parallel", "parallel")),
        name="ret_latent" if latent else "ret_ctx",
    )(*args)


def _merge_kernel(ys_ref, z_ref, ng_ref, ya_ref, yr_ref, g0_ref, g1_ref, g2_ref, wb_ref, o_ref, y0_ref):
    @pl.when(pl.program_id(1) == 0)
    def _():
        y = ys_ref[...] * _silu(z_ref[...])
        y0_ref[...] = _rms(y, ng_ref[...]).astype(BF16)

    acc = _sigmoid(g0_ref[...]) * _dot(y0_ref[...], wb_ref[0])
    acc += _sigmoid(g1_ref[...]) * _dot(ya_ref[...], wb_ref[1])
    acc += _sigmoid(g2_ref[...]) * _dot(yr_ref[...], wb_ref[2])
    o_ref[...] = acc.astype(BF16)


def _merge(y_ssd, proj, ssd_ng, y_att, y_ret, w_branch, layer, tm=512, tn=512):
    t = y_ssd.shape[0]
    gl = OFF_GL // tn
    nj = D // tn
    return pl.pallas_call(
        _merge_kernel,
        grid=(t // tm, nj),
        in_specs=[pl.BlockSpec((tm, D), lambda i, j: (i, 0)),
                  pl.BlockSpec((tm, D), lambda i, j: (i, OFF_Z // D)),
                  pl.BlockSpec((None, 1, D), lambda i, j: (layer, 0, 0)),
                  pl.BlockSpec((tm, D), lambda i, j: (i, 0)),
                  pl.BlockSpec((tm, D), lambda i, j: (i, 0)),
                  pl.BlockSpec((tm, tn), lambda i, j: (i, gl + j)),
                  pl.BlockSpec((tm, tn), lambda i, j: (i, gl + nj + j)),
                  pl.BlockSpec((tm, tn), lambda i, j: (i, gl + 2 * nj + j)),
                  pl.BlockSpec((None, 3, D, tn), lambda i, j: (layer, 0, 0, j))],
        out_specs=pl.BlockSpec((tm, tn), lambda i, j: (i, j)),
        out_shape=jax.ShapeDtypeStruct((t, D), BF16),
        scratch_shapes=[pltpu.VMEM((tm, D), BF16)],
        compiler_params=_cp(("parallel", "arbitrary")),
        name="merge",
    )(y_ssd, proj, ssd_ng, y_att, y_ret, proj, proj, proj, w_branch)


def _outproj_kernel(m_ref, w_ref, x_ref, g1_ref, ng_ref, sh_ref, sc_ref, wr_ref,
                    x1_ref, h2_ref, aff_ref, afft_ref):
    x1 = x_ref[...] + g1_ref[...] * _dot(m_ref[...], w_ref[...])
    x1_ref[...] = x1
    h = _rms(x1, ng_ref[...]) * (1.0 + sc_ref[...]) + sh_ref[...]
    hb = h.astype(BF16)
    h2_ref[...] = hb
    hl = (h - hb.astype(F32)).astype(BF16)
    logits = _dot(hb, wr_ref[0]) + _dot(hl, wr_ref[0]) + _dot(hb, wr_ref[1])
    lane = lax.broadcasted_iota(jnp.int32, logits.shape, 1)
    logits = jnp.where(lane < N_EXP, logits, -jnp.inf)
    e = jnp.exp(logits - jnp.max(logits, axis=-1, keepdims=True))
    aff = e / jnp.sum(e, axis=-1, keepdims=True)
    aff_ref[...] = aff
    afft_ref[...] = aff.T[0:N_EXP, :]


def _outproj(merged, w_out, x, mods, norm_g, w_router, layer, mod_row, tm=512):
    t = x.shape[0]

    def ms(col):
        return pl.BlockSpec((None, None, 1, D), lambda i: (layer, mod_row(i, tm), 0, col))

    return pl.pallas_call(
        _outproj_kernel,
        grid=(t // tm,),
        in_specs=[pl.BlockSpec((tm, D), lambda i: (i, 0)),
                  pl.BlockSpec((None, D, D), lambda i: (layer, 0, 0)),
                  pl.BlockSpec((tm, D), lambda i: (i, 0)),
                  ms(2),
                  pl.BlockSpec((None, 1, D), lambda i: (layer, 0, 0)),
                  ms(3), ms(4),
                  pl.BlockSpec((None, 2, D, 128), lambda i: (layer, 0, 0, 0))],
        out_specs=[pl.BlockSpec((tm, D), lambda i: (i, 0)),
                   pl.BlockSpec((tm, D), lambda i: (i, 0)),
                   pl.BlockSpec((tm, 128), lambda i: (i, 0)),
                   pl.BlockSpec((N_EXP, tm), lambda i: (0, i))],
        out_shape=[jax.ShapeDtypeStruct((t, D), F32),
                   jax.ShapeDtypeStruct((t, D), BF16),
                   jax.ShapeDtypeStruct((t, 128), F32),
                   jax.ShapeDtypeStruct((N_EXP, t), F32)],
        compiler_params=_cp(("parallel",)),
        name="out_proj",
    )(merged, w_out, x, mods, norm_g, mods, mods, w_router)


SUB = 128
WIN = SUB + 16


def _select_kernel(a_ref, post_ref, postok_ref, st_ref, cnt_ref, *, groups):
    ri = lax.broadcasted_iota(jnp.int32, (SUB, SUB), 0)
    ci = lax.broadcasted_iota(jnp.int32, (SUB, SUB), 1)
    before = jnp.where(ri < ci, 1.0, 0.0).astype(BF16)
    lane = lax.broadcasted_iota(jnp.int32, (N_EXP, SUB), 1)
    st_ref[...] = jnp.zeros_like(st_ref)
    for lo, n, cap, base in groups:
        bits = pltpu.bitcast(a_ref[:, lo:lo + n], jnp.int32)

        def search(i, prefix, bits=bits, cap=cap):
            cand = prefix | jnp.left_shift(jnp.int32(1), 30 - i)
            cnt = jnp.sum(jnp.where(bits >= cand, 1.0, 0.0), axis=1, keepdims=True)
            return jnp.where(cnt >= cap, cand, prefix)

        zero = pltpu.bitcast(jnp.minimum(jnp.min(a_ref[:, lo:lo + n], axis=1, keepdims=True), 0.0), jnp.int32)
        tau = lax.fori_loop(0, 31, search, zero)
        need = cap - jnp.sum(jnp.where(bits > tau, 1.0, 0.0), axis=1, keepdims=True)

        cnt_ref[...] = jnp.zeros_like(cnt_ref)

        def block(b, carry, lo=lo, base=base, tau=tau, need=need):
            ceq = cnt_ref[0][:, 0:1]
            csel = cnt_ref[1][:, 0:1]
            t0 = pl.multiple_of(lo + b * SUB, SUB)
            bb = pltpu.bitcast(a_ref[:, pl.ds(t0, SUB)], jnp.int32)
            eq = jnp.where(bb == tau, 1.0, 0.0)
            rank_eq = ceq + _dot(eq.astype(BF16), before)
            sel = jnp.where((bb > tau) | ((bb == tau) & (rank_eq < need)), 1.0, 0.0)
            pos = csel + _dot(sel.astype(BF16), before) + base
            pm = jnp.where(sel > 0.0, pos, -1.0)
            post_ref[:, pl.ds(t0, SUB)] = pm
            full = jnp.concatenate([pm, jnp.full((SUB - N_EXP, SUB), -1.0, F32)], axis=0)
            postok_ref[pl.ds(t0, SUB), :] = full.T
            st_ref[...] = jnp.where(lane == lo // SUB + b, (csel + base).astype(jnp.int32), st_ref[...])
            cnt_ref[0] = jnp.broadcast_to(ceq + jnp.sum(eq, axis=1, keepdims=True), (N_EXP, SUB))
            cnt_ref[1] = jnp.broadcast_to(csel + jnp.sum(sel, axis=1, keepdims=True), (N_EXP, SUB))
            return carry

        lax.fori_loop(0, n // SUB, block, 0)


def _select(afft, groups):
    t = afft.shape[1]
    assert t // SUB <= SUB
    return pl.pallas_call(
        functools.partial(_select_kernel, groups=groups),
        out_shape=[jax.ShapeDtypeStruct((N_EXP, t), F32),
                   jax.ShapeDtypeStruct((t, SUB), F32),
                   jax.ShapeDtypeStruct((N_EXP, SUB), jnp.int32)],
        scratch_shapes=[pltpu.VMEM((2, N_EXP, SUB), F32)],
        compiler_params=pltpu.CompilerParams(vmem_limit_bytes=VMEM_LIMIT),
        name="select",
    )(afft)


def _ffn_kernel(st_ref, h_ref, pos_ref, w1_ref, w3_ref, w2_ref, o_ref, stage_ref, hid_ref,
                *, n_e, nbd, nf, n2, cap, tbd, tf):
    r = pl.program_id(0)
    s = pl.program_id(1)

    @pl.when((r < n_e) & (s < nbd))
    def _():
        slot = r % 2

        @pl.when(s == 0)
        def _():
            stage_ref[slot] = jnp.zeros(stage_ref.shape[1:], BF16)

        srow = lax.broadcasted_iota(jnp.int32, (WIN, SUB), 0).astype(F32)
        for k in range(tbd // SUB):
            off16 = pl.multiple_of((st_ref[r, s * (tbd // SUB) + k] // 16) * 16, 16)
            rel = pos_ref[:, k * SUB:(k + 1) * SUB] - off16.astype(F32)
            onehot = jnp.where(srow == rel, 1.0, 0.0).astype(BF16)
            rows = _dot(onehot, h_ref[k * SUB:(k + 1) * SUB, :]).astype(BF16)
            stage_ref[slot, pl.ds(off16, WIN), :] += rows

    @pl.when((r > 0) & (s < nf))
    def _():
        x = stage_ref[(r + 1) % 2, 0:cap, :]
        hid = _silu(_dot(x, w1_ref[...].astype(BF16))) * _dot(x, w3_ref[...].astype(BF16))
        hid_ref[:, pl.ds(pl.multiple_of(s * tf, tf), tf)] = hid.astype(BF16)

    @pl.when((r > 0) & (s >= nf) & (s < nf + n2))
    def _():
        o_ref[...] = _dot(hid_ref[...], w2_ref[...].astype(BF16)).astype(BF16)


def _ffn(starts, h2, post, w1, w3, w2, layer, cap, tf=256, tn=256):
    t = h2.shape[0]
    n_e, ff = w1.shape[1], w1.shape[-1]
    nf = ff // tf
    n2 = D // tn
    tbd = max(SUB, (t // (nf + n2)) // SUB * SUB)
    while t % tbd:
        tbd -= SUB
    nbd = t // tbd
    steps = max(nbd, nf + n2)

    def tok(r, s):
        return jnp.where(r < n_e, jnp.minimum(s, nbd - 1), nbd - 1)

    def prev(r):
        return jnp.maximum(r - 1, 0)

    def fcol(s):
        return jnp.minimum(s, nf - 1)

    def ocol(r, s):
        return jnp.where(r > 0, jnp.clip(s - nf, 0, n2 - 1), 0)

    grid_spec = pltpu.PrefetchScalarGridSpec(
        num_scalar_prefetch=1,
        grid=(n_e + 1, steps),
        in_specs=[pl.BlockSpec((tbd, D), lambda r, s, st: (tok(r, s), 0)),
                  pl.BlockSpec((None, 1, tbd), lambda r, s, st: (jnp.minimum(r, n_e - 1), 0, tok(r, s))),
                  pl.BlockSpec((None, None, D, tf), lambda r, s, st: (layer, prev(r), 0, fcol(s))),
                  pl.BlockSpec((None, None, D, tf), lambda r, s, st: (layer, prev(r), 0, fcol(s))),
                  pl.BlockSpec((None, None, ff, tn), lambda r, s, st: (layer, prev(r), 0, ocol(r, s)))],
        out_specs=pl.BlockSpec((None, cap, tn), lambda r, s, st: (prev(r), 0, ocol(r, s))),
        scratch_shapes=[pltpu.VMEM((2, cap + WIN, D), BF16), pltpu.VMEM((cap, ff), BF16)],
    )
    return pl.pallas_call(
        functools.partial(_ffn_kernel, n_e=n_e, nbd=nbd, nf=nf, n2=n2, cap=cap, tbd=tbd, tf=tf),
        grid_spec=grid_spec,
        out_shape=jax.ShapeDtypeStruct((n_e, cap, D), BF16),
        compiler_params=_cp(("arbitrary", "arbitrary")),
        name="expert_ffn",
    )(starts, h2, post.reshape(n_e, 1, t), w1, w3, w2)


def _combine_kernel(st_ref, ye_hbm, postok_ref, aff_ref, x1_ref, g2_ref, o_ref, win_ref, sem, *, cap):
    b = pl.program_id(0)
    nb = pl.num_programs(0)

    def first_slot(bb, e):
        return pl.multiple_of(jnp.minimum((st_ref[e, bb] // 16) * 16, cap - WIN), 16)

    def window_copy(bb, slot, e):
        return pltpu.make_async_copy(ye_hbm.at[e, pl.ds(first_slot(bb, e), WIN), :],
                                     win_ref.at[slot, pl.ds(e * WIN, WIN), :], sem.at[slot])

    @pl.when(b == 0)
    def _():
        tail = jnp.zeros((2 * SUB - WIN, D), BF16)
        win_ref[0, N_EXP * WIN:, :] = tail
        win_ref[1, N_EXP * WIN:, :] = tail
        for e in range(N_EXP):
            window_copy(0, 0, e).start()

    @pl.when(b + 1 < nb)
    def _():
        for e in range(N_EXP):
            window_copy(b + 1, (b + 1) % 2, e).start()

    slot = b % 2
    for e in range(N_EXP):
        window_copy(b, slot, e).wait()

    lane = lax.broadcasted_iota(jnp.int32, (SUB, 2 * SUB), 1).astype(F32)
    acc = jnp.zeros((SUB, D), F32)
    for e in range(N_EXP):
        rel = postok_ref[:, e:e + 1] - first_slot(b, e).astype(F32)
        gated = jnp.where(lane == rel, aff_ref[:, e:e + 1], 0.0).astype(BF16)
        acc += _dot(gated, win_ref[slot, pl.ds(e * WIN, 2 * SUB), :])
    o_ref[...] = x1_ref[...] + g2_ref[...] * acc


def _combine(starts, ye, postok, aff, x1, mods, layer, mod_row):
    t = x1.shape[0]
    n_e, cap, _ = ye.shape
    grid_spec = pltpu.PrefetchScalarGridSpec(
        num_scalar_prefetch=1,
        grid=(t // SUB,),
        in_specs=[pl.BlockSpec(memory_space=pl.ANY),
                  pl.BlockSpec((SUB, SUB), lambda b, st: (b, 0)),
                  pl.BlockSpec((SUB, SUB), lambda b, st: (b, 0)),
                  pl.BlockSpec((SUB, D), lambda b, st: (b, 0)),
                  pl.BlockSpec((None, None, 1, D), lambda b, st: (layer, mod_row(b, SUB), 0, 5))],
        out_specs=pl.BlockSpec((SUB, D), lambda b, st: (b, 0)),
        scratch_shapes=[pltpu.VMEM((2, N_EXP * WIN + 2 * SUB - WIN, D), BF16),
                        pltpu.SemaphoreType.DMA((2,))],
    )
    return pl.pallas_call(
        functools.partial(_combine_kernel, cap=cap),
        grid_spec=grid_spec,
        out_shape=jax.ShapeDtypeStruct((t, D), F32),
        compiler_params=_cp(("arbitrary",)),
        name="combine",
    )(starts, ye, postok, aff, x1, mods)


def _rope_tables(n_tokens):
    rows_n = n_tokens // GRID_W
    row = jnp.repeat(jnp.arange(rows_n, dtype=F32), GRID_W)
    col = jnp.tile(jnp.arange(GRID_W, dtype=F32), rows_n)
    n_freq = ATT_D // 4
    inv = ROPE_THETA ** (-jnp.arange(n_freq, dtype=F32) / n_freq)
    ang_r = row[:, None] * inv
    ang_c = col[:, None] * inv
    zeros = jnp.zeros_like(ang_r)
    cos = jnp.concatenate([jnp.cos(ang_r)] * 2 + [jnp.cos(ang_c)] * 2, axis=1)
    sa = jnp.concatenate([-jnp.sin(ang_r), zeros, -jnp.sin(ang_c), zeros], axis=1)
    sb = jnp.concatenate([zeros, jnp.sin(ang_r), zeros, jnp.sin(ang_c)], axis=1)
    return cos, sa, sb


def _group_dt_lanes(v):
    depth = v.shape[0]
    v = v.reshape(depth, 2, SSD_G, SSD_R).transpose(0, 2, 1, 3).reshape(depth, SSD_G, 1, 2 * SSD_R)
    return jnp.pad(v, ((0, 0), (0, 0), (0, 0), (0, 128 - 2 * SSD_R)))


def _prep_w_in(w_in):
    depth = w_in.shape[0]
    dt = w_in[:, :, ORIG_DT:ORIG_AFTER_DT].reshape(depth, D, 2, SSD_G, SSD_R)
    dt = dt.transpose(0, 1, 3, 2, 4).reshape(depth, D, SSD_G, 2 * SSD_R)
    dt = jnp.pad(dt, ((0, 0), (0, 0), (0, 0), (0, 128 - 2 * SSD_R))).reshape(depth, D, SSD_G * 128)
    w = jnp.concatenate([w_in[:, :, :ORIG_DT], w_in[:, :, ORIG_AFTER_DT:], dt], axis=2)
    return w.astype(BF16)


def kernel(x_prompt, x_sample, cache_attn_k, cache_attn_v, state_ssd_fwd, state_ssd_bwd, state_ret_fwd, state_ret_bwd, c, c_ctx, norm1_g, norm2_g, w_ada, b_ada, w_in, ssd_conv_w, ssd_conv_b, ssd_a_log, ssd_dt_bias, ssd_d, ssd_norm_g, q_norm_g, k_norm_g, ret_decay, ret_norm_g, w_branch, w_out, w_router, w_e1, w_e3, w_e2):
    depth = w_in.shape[0]
    pb, pseq, _ = x_prompt.shape
    sb, sseq, _ = x_sample.shape
    n_p = pb * pseq
    n_s = sb * sseq
    assert n_p % 1024 == 0 and sseq % 1024 == 0 and 1 + sb <= 8

    def mod_row(i, tm):
        return jnp.where(i < n_p // tm, 0, 1 + (i * tm - n_p) // sseq)

    w_in_b = _prep_w_in(w_in)
    w_branch_b = w_branch.astype(BF16)
    w_out_b = w_out.astype(BF16)
    cap_p = CAP_FACTOR * n_p // N_EXP
    cap_s = CAP_FACTOR * n_s // N_EXP
    groups = ((0, n_p, cap_p, 0), (n_p, n_s, cap_s, cap_p))
    w_router_f = jnp.pad(w_router, ((0, 0), (0, 0), (0, 128 - N_EXP)))
    w_router_hi = w_router_f.astype(BF16)
    w_router_p = jnp.stack([w_router_hi, (w_router_f - w_router_hi.astype(F32)).astype(BF16)], axis=1)
    dtb = _group_dt_lanes(ssd_dt_bias)
    alog = _group_dt_lanes(ssd_a_log)
    dvec = jnp.repeat(ssd_d, SSD_P, axis=1).reshape(depth, 1, D)
    rd = jnp.broadcast_to(ret_decay.transpose(0, 2, 1)[..., None], (depth, RET_H, 2, 128))
    rope = _rope_tables(sseq)
    n1g = norm1_g.reshape(depth, 1, D)
    n2g = norm2_g.reshape(depth, 1, D)
    ssd_ng = ssd_norm_g.reshape(depth, 1, D)
    ret_ng = ret_norm_g.reshape(depth, 1, D)
    qg = q_norm_g.reshape(depth, 1, ATT_D)
    kg = k_norm_g.reshape(depth, 1, ATT_D)

    cond8 = jnp.zeros((8, D), F32).at[0].set(c_ctx).at[1:1 + sb].set(c)
    mods = _ada(cond8, w_ada, b_ada).reshape(depth, 8, 1, 6 * D)

    x = jnp.concatenate([x_prompt.reshape(n_p, D), x_sample.reshape(n_s, D)], axis=0)
    new_k, new_v, new_hf, new_hb, new_sf, new_sb = [], [], [], [], [], []
    y_ssd = jnp.zeros((n_p + n_s, D), F32)
    y_att = jnp.zeros((n_p + n_s, D), BF16)
    y_ret = jnp.zeros((n_p + n_s, D), BF16)
    for l in range(depth):
        proj = _inproj(x, n1g, mods, w_in_b, l, mod_row)

        ys_p, hf, hb = _ssd(proj, 0, pb, pseq, ssd_conv_w, ssd_conv_b.reshape(depth, 1, -1), dtb, alog, dvec,
                            l, out_state=True, dst=y_ssd)
        (y_ssd,) = _ssd(proj, n_p, sb, sseq, ssd_conv_w, ssd_conv_b.reshape(depth, 1, -1), dtb, alog, dvec,
                        l, h0=(state_ssd_fwd, state_ssd_bwd), dst=ys_p)
        ya_p, kn, vn = _att(proj, 0, pb, pseq, qg, kg, l, tq=pseq, dst=y_att)
        (y_att,) = _att(proj, n_p, sb, sseq, qg, kg, l, tq=512, ctx=(cache_attn_k, cache_attn_v), rope=rope,
                        dst=ya_p)
        yr_p, sf, sbk = _ret(proj, 0, pb, pseq, rd, ret_ng, l, out_state=True, dst=y_ret)
        (y_ret,) = _ret(proj, n_p, sb, sseq, rd, ret_ng, l, rope=rope, s0=(state_ret_fwd, state_ret_bwd),
                        dst=yr_p)
        new_k.append(kn), new_v.append(vn), new_hf.append(hf), new_hb.append(hb)
        new_sf.append(sf), new_sb.append(sbk)

        merged = _merge(y_ssd, proj, ssd_ng, y_att, y_ret, w_branch_b, l)
        x1, h2, aff, afft = _outproj(merged, w_out_b, x, mods, n2g, w_router_p, l, mod_row)
        post, postok, starts = _select(afft, groups)
        ye = _ffn(starts, h2, post, w_e1, w_e3, w_e2, l, cap_p + cap_s)
        x = _combine(starts, ye, postok, aff, x1, mods, l, mod_row)

    y_prompt = x[:n_p].reshape(pb, pseq, D)
    y_sample = x[n_p:].reshape(sb, sseq, D)
    return (y_prompt, y_sample, jnp.stack(new_k, axis=1), jnp.stack(new_v, axis=1),
            jnp.stack(new_hf, axis=1), jnp.stack(new_hb, axis=1),
            jnp.stack(new_sf, axis=1), jnp.stack(new_sb, axis=1))
```

```python
import functools

import jax
import jax.numpy as jnp
from jax import lax
from jax.experimental import pallas as pl
from jax.experimental.pallas import tpu as pltpu

F32 = jnp.float32
BF16 = jnp.bfloat16
EPS = 1e-6
ROPE_THETA = 10000.0
GRID_W = 64

D = 2048
CH = 128
SSD_P = 64
SSD_N = 128
SSD_G = 4
SSD_H = 32
SSD_R = SSD_H // SSD_G
ATT_D = 128
ATT_H = 16
ATT_G = 4
ATT_R = ATT_H // ATT_G
RET_H = 8
RET_K = 128
RET_V = 256
N_EXP = 16
CAP_FACTOR = 2

OFF_Z = 0
OFF_X = 2048
OFF_B = 4096
OFF_C = 4608
OFF_AQ = 5120
OFF_AK = 7168
OFF_AV = 7680
OFF_RQ = 8192
OFF_RK = 9216
OFF_RV = 10240
OFF_RG = 12288
OFF_GL = 14336
OFF_DT = 20480
NW = OFF_DT + SSD_G * 128
ORIG_DT = 5120
ORIG_AFTER_DT = 5184
ORIG_W = 20544

VMEM_LIMIT = 56 * 1024 * 1024


def _cp(sem):
    return pltpu.CompilerParams(dimension_semantics=sem, vmem_limit_bytes=VMEM_LIMIT)


def _sigmoid(x):
    return 1.0 / (1.0 + jnp.exp(-x))


def _silu(x):
    return x * _sigmoid(x)


def _softplus(x):
    return jnp.maximum(x, 0.0) + jnp.log(1.0 + jnp.exp(-jnp.abs(x)))


def _rms(x, g):
    ms = jnp.mean(x * x, axis=-1, keepdims=True)
    return x * lax.rsqrt(ms + EPS) * g


def _dot(a, b):
    return jnp.dot(a, b, preferred_element_type=F32)


def _dot_nt(a, b):
    return lax.dot_general(a, b, (((1,), (1,)), ((), ())), preferred_element_type=F32)


def _rope(x, cos, sa, sb):
    return x * cos + pltpu.roll(x, 96, 1) * sa + pltpu.roll(x, 32, 1) * sb


def _ada_kernel(c_ref, w_ref, b_ref, o_ref):
    s = _silu(c_ref[...]).astype(BF16)
    o_ref[...] = _dot(s, w_ref[...].astype(BF16)) + b_ref[...]


def _ada(cond8, w_ada, b_ada):
    depth, _, n6 = w_ada.shape
    tn = 1024
    return pl.pallas_call(
        _ada_kernel,
        grid=(depth, n6 // tn),
        in_specs=[pl.BlockSpec((8, D), lambda l, j: (0, 0)),
                  pl.BlockSpec((None, D, tn), lambda l, j: (l, 0, j)),
                  pl.BlockSpec((None, 1, tn), lambda l, j: (l, 0, j))],
        out_specs=pl.BlockSpec((None, 8, tn), lambda l, j: (l, 0, j)),
        out_shape=jax.ShapeDtypeStruct((depth, 8, n6), F32),
        compiler_params=_cp(("parallel", "parallel")),
        name="ada_mod",
    )(cond8, w_ada, b_ada.reshape(depth, 1, n6))


def _inproj_kernel(x_ref, g_ref, sh_ref, sc_ref, w_ref, o_ref, hn_ref):
    @pl.when(pl.program_id(1) == 0)
    def _():
        h = _rms(x_ref[...], g_ref[...])
        hn_ref[...] = (h * (1.0 + sc_ref[...]) + sh_ref[...]).astype(BF16)

    o_ref[...] = _dot(hn_ref[...], w_ref[...])


def _inproj(x, norm_g, mods, w_in, layer, mod_row, tm=2048, tn=512):
    t = x.shape[0]
    return pl.pallas_call(
        _inproj_kernel,
        grid=(t // tm, NW // tn),
        in_specs=[pl.BlockSpec((tm, D), lambda i, j: (i, 0), pipeline_mode=pl.Buffered(1)),
                  pl.BlockSpec((None, 1, D), lambda i, j: (layer, 0, 0)),
                  pl.BlockSpec((None, None, 1, D), lambda i, j: (layer, mod_row(i, tm), 0, 0)),
                  pl.BlockSpec((None, None, 1, D), lambda i, j: (layer, mod_row(i, tm), 0, 1)),
                  pl.BlockSpec((None, D, tn), lambda i, j: (layer, 0, j))],
        out_specs=pl.BlockSpec((tm, tn), lambda i, j: (i, j)),
        out_shape=jax.ShapeDtypeStruct((t, NW), F32),
        scratch_shapes=[pltpu.VMEM((tm, D), BF16)],
        compiler_params=_cp(("parallel", "arbitrary")),
        name="in_proj",
    )(x, norm_g, mods, mods, w_in)


def _ssd_kernel(*refs, seq, has_h0, out_state, has_dst):
    it = iter(refs)
    x_ref, b_ref, c_ref, dt_ref = next(it), next(it), next(it), next(it)
    cwx_ref, cwb_ref, cwc_ref = next(it), next(it), next(it)
    cbx_ref, cbb_ref, cbc_ref = next(it), next(it), next(it)
    dtb_ref, alog_ref, dvec_ref = next(it), next(it), next(it)
    h0_refs = (next(it), next(it)) if has_h0 else None
    if has_dst:
        next(it)
    y_ref = next(it)
    hout_refs = (next(it), next(it)) if out_state else None
    xs_ref, bs_ref, cs_ref, st_ref = next(it), next(it), next(it), next(it)

    nc = seq // CH
    row1 = lax.broadcasted_iota(jnp.int32, (CH, 1), 0)

    def conv_chunk(c, carry):
        t0 = pl.multiple_of(c * CH, CH)
        pidx = pl.multiple_of(jnp.maximum(t0 - 8, 0), 8)
        nidx = pl.multiple_of(jnp.minimum(t0 + CH, seq - 8), 8)

        def conv(src, w_ref, bias_ref, dst):
            cur = src[pl.ds(t0, CH), :]
            prev_row = jnp.where(c > 0, src[pl.ds(pidx, 8), :][7:8, :], 0.0)
            next_row = jnp.where(c < nc - 1, src[pl.ds(nidx, 8), :][0:1, :], 0.0)
            xm = jnp.where(row1 == 0, prev_row, pltpu.roll(cur, 1, 0))
            xp = jnp.where(row1 == CH - 1, next_row, pltpu.roll(cur, CH - 1, 0))
            w = w_ref[...]
            v = xm * w[0:1, :] + cur * w[1:2, :] + xp * w[2:3, :] + bias_ref[...]
            v = _silu(v)
            dst[pl.ds(t0, CH), :] = v
            return v

        xs = conv(x_ref, cwx_ref, cbx_ref, xs_ref)
        y_ref[pl.ds(t0, CH), :] = dvec_ref[...] * xs
        conv(b_ref, cwb_ref, cbb_ref, bs_ref)
        conv(c_ref, cwc_ref, cbc_ref, cs_ref)
        return carry

    lax.fori_loop(0, nc, conv_chunk, 0)

    ri = lax.broadcasted_iota(jnp.int32, (CH, CH), 0)
    ci = lax.broadcasted_iota(jnp.int32, (CH, CH), 1)
    tri = (ri >= ci).astype(F32)
    left = ci < SSD_P
    left_row = ci[0:1, :] < SSD_P
    dtb = dtb_ref[...]
    aneg = -jnp.exp(alog_ref[...])

    for d in range(2):
        for pair in range(SSD_R // 2):
            if has_h0:
                st_ref[d, pair] = h0_refs[d][2 * pair:2 * pair + 2].reshape(2 * SSD_P, SSD_N).T
            else:
                st_ref[d, pair] = jnp.zeros((SSD_N, 2 * SSD_P), F32)

    def scan_one(d, ch):
        mask = (ri >= ci) if d == 0 else (ci >= ri)
        t0 = pl.multiple_of(ch * CH, CH)
        dtv = _softplus(dt_ref[pl.ds(t0, CH), :] + dtb)
        a = dtv * aneg
        pref = jnp.dot(tri, a, precision=lax.Precision.HIGHEST, preferred_element_type=F32)
        tot = pref[CH - 1:CH, :]
        s = pref if d == 0 else tot - pref + a
        cd = jnp.exp(tot)
        s_t = s.T
        dt_t = dtv.T
        bc = bs_ref[pl.ds(t0, CH), :]
        bc_t = bc.T
        bcb = bc.astype(BF16)
        ccb = cs_ref[pl.ds(t0, CH), :].astype(BF16)
        cb = _dot_nt(ccb, bcb)
        for pair in range(SSD_R // 2):
            la = d * SSD_R + 2 * pair
            lb = la + 1
            x2 = xs_ref[pl.ds(t0, CH), pair * 128:(pair + 1) * 128]
            x_a = jnp.where(left, x2, 0.0).astype(BF16)
            x_b = jnp.where(left, 0.0, x2).astype(BF16)
            sa = jnp.broadcast_to(s[:, la:la + 1], (CH, CH))
            sb = jnp.broadcast_to(s[:, lb:lb + 1], (CH, CH))
            sa_t, sb_t = s_t[la:la + 1, :], s_t[lb:lb + 1, :]
            dta_t, dtb_t = dt_t[la:la + 1, :], dt_t[lb:lb + 1, :]
            w_a = (cb * jnp.exp(jnp.where(mask, sa - sa_t, -jnp.inf)) * dta_t).astype(BF16)
            w_b = (cb * jnp.exp(jnp.where(mask, sb - sb_t, -jnp.inf)) * dtb_t).astype(BF16)
            st = st_ref[d, pair]
            y = _dot(w_a, x_a) + _dot(w_b, x_b)
            y = y + _dot(ccb, st.astype(BF16)) * jnp.exp(jnp.where(left, sa, sb))
            g_a = jnp.exp(tot[:, la:la + 1] - sa_t) * dta_t
            g_b = jnp.exp(tot[:, lb:lb + 1] - sb_t) * dtb_t
            new = _dot((bc_t * g_a).astype(BF16), x_a) + _dot((bc_t * g_b).astype(BF16), x_b)
            cd2 = jnp.where(left_row, cd[:, la:la + 1], cd[:, lb:lb + 1])
            st_ref[d, pair] = cd2 * st + new
            y_ref[pl.ds(t0, CH), pair * 128:(pair + 1) * 128] += y

    def scan_chunk(c, carry):
        scan_one(0, c)
        scan_one(1, nc - 1 - c)
        return carry

    lax.fori_loop(0, nc, scan_chunk, 0, unroll=min(nc, 4))
    if out_state:
        for d in range(2):
            for pair in range(SSD_R // 2):
                hout_refs[d][2 * pair:2 * pair + 2] = st_ref[d, pair].T.reshape(2, SSD_P, SSD_N)


def _ssd(proj, row_off, nb, seq, conv_w, conv_b, dtb, alog, dvec, layer, h0=None, out_state=False, dst=None):
    rb = row_off // seq
    has_h0 = h0 is not None
    in_specs = [
        pl.BlockSpec((seq, 512), lambda b, g: (rb + b, OFF_X // 512 + g)),
        pl.BlockSpec((seq, 128), lambda b, g: (rb + b, OFF_B // 128 + g)),
        pl.BlockSpec((seq, 128), lambda b, g: (rb + b, OFF_C // 128 + g)),
        pl.BlockSpec((seq, 128), lambda b, g: (rb + b, OFF_DT // 128 + g)),
        pl.BlockSpec((None, 3, 512), lambda b, g: (layer, 0, g)),
        pl.BlockSpec((None, 3, 128), lambda b, g: (layer, 0, D // 128 + g)),
        pl.BlockSpec((None, 3, 128), lambda b, g: (layer, 0, D // 128 + SSD_G + g)),
        pl.BlockSpec((None, 1, 512), lambda b, g: (layer, 0, g)),
        pl.BlockSpec((None, 1, 128), lambda b, g: (layer, 0, D // 128 + g)),
        pl.BlockSpec((None, 1, 128), lambda b, g: (layer, 0, D // 128 + SSD_G + g)),
        pl.BlockSpec((None, None, 1, 128), lambda b, g: (layer, g, 0, 0)),
        pl.BlockSpec((None, None, 1, 128), lambda b, g: (layer, g, 0, 0)),
        pl.BlockSpec((None, 1, 512), lambda b, g: (layer, 0, g)),
    ]
    args = [proj, proj, proj, proj, conv_w, conv_w, conv_w, conv_b, conv_b, conv_b, dtb, alog, dvec]
    if has_h0:
        st_spec = pl.BlockSpec((None, None, SSD_R, SSD_P, SSD_N), lambda b, g: (b, layer, g, 0, 0))
        in_specs += [st_spec, st_spec]
        args += [h0[0], h0[1]]
    aliases = {}
    if dst is not None:
        in_specs.append(pl.BlockSpec(memory_space=pl.ANY))
        args.append(dst)
        aliases = {len(args) - 1: 0}
    out_specs = [pl.BlockSpec((seq, 512), lambda b, g: (rb + b, g))]
    out_shape = [jax.ShapeDtypeStruct((proj.shape[0], D), F32)]
    if out_state:
        so = pl.BlockSpec((None, SSD_R, SSD_P, SSD_N), lambda b, g: (b, g, 0, 0))
        out_specs += [so, so]
        out_shape += [jax.ShapeDtypeStruct((nb, SSD_H, SSD_P, SSD_N), F32)] * 2
    return pl.pallas_call(
        functools.partial(_ssd_kernel, seq=seq, has_h0=has_h0, out_state=out_state, has_dst=dst is not None),
        input_output_aliases=aliases,
        grid=(nb, SSD_G),
        in_specs=in_specs,
        out_specs=out_specs,
        out_shape=out_shape,
        scratch_shapes=[pltpu.VMEM((seq, 512), F32), pltpu.VMEM((seq, 128), F32),
                        pltpu.VMEM((seq, 128), F32), pltpu.VMEM((2, SSD_R // 2, 128, 128), F32)],
        compiler_params=_cp(("parallel", "parallel")),
        name="ssd_latent" if has_h0 else "ssd_ctx",
    )(*args)


def _att_kernel(*refs, seq, past, tq, latent, has_dst):
    it = iter(refs)
    q_ref, k_ref, v_ref, qg_ref, kg_ref = next(it), next(it), next(it), next(it), next(it)
    if latent:
        kc_ref, vc_ref, cos_ref, sa_ref, sb_ref = next(it), next(it), next(it), next(it), next(it)
    if has_dst:
        next(it)
    o_ref = next(it)
    if not latent:
        ko_ref, vo_ref = next(it), next(it)
    kb_ref, vb_ref = next(it), next(it)
    qi = pl.program_id(2)

    @pl.when(qi == 0)
    def _():
        kn = _rms(k_ref[...], kg_ref[...])
        v = v_ref[...]
        if latent:
            kn = _rope(kn, cos_ref[...], sa_ref[...], sb_ref[...])
            kb_ref[0:past, :] = kc_ref[...].astype(BF16)
            vb_ref[0:past, 0:ATT_D] = vc_ref[...].astype(BF16)
            kb_ref[past:past + seq, :] = kn.astype(BF16)
            vb_ref[past:past + seq, 0:ATT_D] = v.astype(BF16)
        else:
            ko_ref[...] = kn
            vo_ref[...] = v
            kb_ref[...] = kn.astype(BF16)
            vb_ref[:, 0:ATT_D] = v.astype(BF16)
        vb_ref[:, ATT_D:2 * ATT_D] = jnp.ones((past + seq, ATT_D), BF16)

    scale = ATT_D ** -0.5 * 1.4426950408889634
    kb = kb_ref[...]
    vb = vb_ref[...]
    if latent:
        q0 = pl.multiple_of(qi * tq, tq)
        cos = cos_ref[pl.ds(q0, tq), :]
        sa = sa_ref[pl.ds(q0, tq), :]
        sb = sb_ref[pl.ds(q0, tq), :]
    for r in range(ATT_R):
        qn = _rms(q_ref[:, r * ATT_D:(r + 1) * ATT_D], qg_ref[...])
        if latent:
            qn = _rope(qn, cos, sa, sb)
        s = _dot_nt((qn * scale).astype(BF16), kb)
        m = jnp.max(s, axis=-1, keepdims=True)
        p = jnp.exp2(s - m)
        ol = _dot(p.astype(BF16), vb)
        o = ol[:, 0:ATT_D] / ol[:, ATT_D:ATT_D + 1]
        o_ref[:, r * ATT_D:(r + 1) * ATT_D] = o.astype(BF16)


def _att(proj, row_off, nb, seq, qg, kg, layer, tq, ctx=None, rope=None, dst=None):
    latent = ctx is not None
    past = ctx[0].shape[3] if latent else 0
    nq = seq // tq
    rbq = row_off // tq
    rbs = row_off // seq
    in_specs = [
        pl.BlockSpec((tq, 512), lambda b, g, q: (rbq + b * nq + q, OFF_AQ // 512 + g)),
        pl.BlockSpec((seq, 128), lambda b, g, q: (rbs + b, OFF_AK // 128 + g)),
        pl.BlockSpec((seq, 128), lambda b, g, q: (rbs + b, OFF_AV // 128 + g)),
        pl.BlockSpec((None, 1, ATT_D), lambda b, g, q: (layer, 0, 0)),
        pl.BlockSpec((None, 1, ATT_D), lambda b, g, q: (layer, 0, 0)),
    ]
    args = [proj, proj, proj, qg, kg]
    if latent:
        cs = pl.BlockSpec((None, None, None, past, ATT_D), lambda b, g, q: (b, layer, g, 0, 0))
        ts = pl.BlockSpec((seq, ATT_D), lambda b, g, q: (0, 0))
        in_specs += [cs, cs, ts, ts, ts]
        args += [ctx[0], ctx[1], rope[0], rope[1], rope[2]]
    aliases = {}
    if dst is not None:
        in_specs.append(pl.BlockSpec(memory_space=pl.ANY))
        args.append(dst)
        aliases = {len(args) - 1: 0}
    out_specs = [pl.BlockSpec((tq, 512), lambda b, g, q: (rbq + b * nq + q, g))]
    out_shape = [jax.ShapeDtypeStruct((proj.shape[0], D), BF16)]
    if not latent:
        so = pl.BlockSpec((None, None, seq, ATT_D), lambda b, g, q: (b, g, 0, 0))
        out_specs += [so, so]
        out_shape += [jax.ShapeDtypeStruct((nb, ATT_G, seq, ATT_D), F32)] * 2
    return pl.pallas_call(
        functools.partial(_att_kernel, seq=seq, past=past, tq=tq, latent=latent, has_dst=dst is not None),
        input_output_aliases=aliases,
        grid=(nb, ATT_G, nq),
        in_specs=in_specs,
        out_specs=out_specs,
        out_shape=out_shape,
        scratch_shapes=[pltpu.VMEM((past + seq, ATT_D), BF16), pltpu.VMEM((past + seq, 2 * ATT_D), BF16)],
        compiler_params=_cp(("parallel", "parallel", "arbitrary")),
        name="att_latent" if latent else "att_ctx",
    )(*args)


def _ret_kernel(*refs, seq, latent, out_state, has_dst):
    it = iter(refs)
    q_ref, k_ref, v_ref, rg_ref, rd_ref, ng_ref = (next(it) for _ in range(6))
    if latent:
        cos_ref, sa_ref, sb_ref, s0f_ref, s0b_ref = (next(it) for _ in range(5))
    if has_dst:
        next(it)
    o_ref = next(it)
    if out_state:
        sout_refs = (next(it), next(it))
    of_ref, st_ref = next(it), next(it)

    nc = seq // CH
    ri = lax.broadcasted_iota(jnp.int32, (CH, CH), 0)
    ci = lax.broadcasted_iota(jnp.int32, (CH, CH), 1)
    pos = ri.astype(F32)
    diff = (ri - ci).astype(F32)
    scale = RET_K ** -0.5
    rd = rd_ref[...]
    lg_all = jnp.minimum(rd, 0.0) - jnp.log(1.0 + jnp.exp(-jnp.abs(rd)))

    for d in range(2):
        lg = lg_all[d:d + 1, :]
        if d == 0:
            dm = jnp.exp(jnp.where(ri >= ci, diff * lg, -jnp.inf))
            kte = jnp.exp((CH - 1.0 - pos) * lg)
            qfs = jnp.exp((pos + 1.0) * lg)
        else:
            dm = jnp.exp(jnp.where(ci > ri, -diff * lg, -jnp.inf))
            kte = jnp.exp(pos * lg)
            qfs = jnp.exp((CH - pos) * lg)
        cdec = jnp.exp(CH * lg)
        cdec2 = jnp.concatenate([cdec, cdec], axis=1)
        if latent:
            st_ref[...] = (s0f_ref if d == 0 else s0b_ref)[...]
        else:
            st_ref[...] = jnp.zeros_like(st_ref)

        def chunk(c, carry, d=d, dm=dm, kte=kte, qfs=qfs, cdec2=cdec2):
            ch = c if d == 0 else nc - 1 - c
            t0 = pl.multiple_of(ch * CH, CH)
            q = q_ref[pl.ds(t0, CH), :] * scale
            k = k_ref[pl.ds(t0, CH), :]
            if latent:
                cos = cos_ref[pl.ds(t0, CH), :]
                sa = sa_ref[pl.ds(t0, CH), :]
                sb = sb_ref[pl.ds(t0, CH), :]
                q = _rope(q, cos, sa, sb)
                k = _rope(k, cos, sa, sb)
            vb = v_ref[pl.ds(t0, CH), :].astype(BF16)
            sc = _dot_nt(q.astype(BF16), k.astype(BF16)) * dm
            st = st_ref[...]
            o = _dot(sc.astype(BF16), vb) + _dot((q * qfs).astype(BF16), st.astype(BF16))
            st_ref[...] = cdec2 * st + _dot((k * kte).T.astype(BF16), vb)
            if d == 0:
                of_ref[pl.ds(t0, CH), :] = o
            else:
                tot = of_ref[pl.ds(t0, CH), :] + o
                y = _silu(rg_ref[pl.ds(t0, CH), :]) * _rms(tot, ng_ref[...])
                o_ref[pl.ds(t0, CH), :] = y.astype(BF16)
            return carry

        lax.fori_loop(0, nc, chunk, 0, unroll=min(nc, 8))
        if out_state:
            sout_refs[d][...] = st_ref[...]


def _ret(proj, row_off, nb, seq, rd, ng, layer, rope=None, s0=None, out_state=False, dst=None):
    latent = s0 is not None
    rb = row_off // seq
    in_specs = [
        pl.BlockSpec((seq, RET_K), lambda b, h: (rb + b, OFF_RQ // RET_K + h)),
        pl.BlockSpec((seq, RET_K), lambda b, h: (rb + b, OFF_RK // RET_K + h)),
        pl.BlockSpec((seq, RET_V), lambda b, h: (rb + b, OFF_RV // RET_V + h)),
        pl.BlockSpec((seq, RET_V), lambda b, h: (rb + b, OFF_RG // RET_V + h)),
        pl.BlockSpec((None, None, 2, 128), lambda b, h: (layer, h, 0, 0)),
        pl.BlockSpec((None, 1, RET_V), lambda b, h: (layer, 0, h)),
    ]
    args = [proj, proj, proj, proj, rd, ng]
    if latent:
        ts = pl.BlockSpec((seq, RET_K), lambda b, h: (0, 0))
        ss = pl.BlockSpec((None, None, None, RET_K, RET_V), lambda b, h: (b, layer, h, 0, 0))
        in_specs += [ts, ts, ts, ss, ss]
        args += [rope[0], rope[1], rope[2], s0[0], s0[1]]
    aliases = {}
    if dst is not None:
        in_specs.append(pl.BlockSpec(memory_space=pl.ANY))
        args.append(dst)
        aliases = {len(args) - 1: 0}
    out_specs = [pl.BlockSpec((seq, RET_V), lambda b, h: (rb + b, h))]
    out_shape = [jax.ShapeDtypeStruct((proj.shape[0], D), BF16)]
    if out_state:
        so = pl.BlockSpec((None, None, RET_K, RET_V), lambda b, h: (b, h, 0, 0))
        out_specs += [so, so]
        out_shape += [jax.ShapeDtypeStruct((nb, RET_H, RET_K, RET_V), F32)] * 2
    return pl.pallas_call(
        functools.partial(_ret_kernel, seq=seq, latent=latent, out_state=out_state, has_dst=dst is not None),
        input_output_aliases=aliases,
        grid=(nb, RET_H),
        in_specs=in_specs,
        out_specs=out_specs,
        out_shape=out_shape,
        scratch_shapes=[pltpu.VMEM((seq, RET_V), F32), pltpu.VMEM((RET_K, RET_V), F32)],
        compiler_params=_cp(("parallel", "parallel")),
        name="ret_latent" if latent else "ret_ctx",
    )(*args)


def _merge_kernel(ys_ref, z_ref, ng_ref, ya_ref, yr_ref, g0_ref, g1_ref, g2_ref, wb_ref, o_ref, y0_ref):
    @pl.when(pl.program_id(1) == 0)
    def _():
        y = ys_ref[...] * _silu(z_ref[...])
        y0_ref[...] = _rms(y, ng_ref[...]).astype(BF16)

    acc = _sigmoid(g0_ref[...]) * _dot(y0_ref[...], wb_ref[0])
    acc += _sigmoid(g1_ref[...]) * _dot(ya_ref[...], wb_ref[1])
    acc += _sigmoid(g2_ref[...]) * _dot(yr_ref[...], wb_ref[2])
    o_ref[...] = acc.astype(BF16)


def _merge(y_ssd, proj, ssd_ng, y_att, y_ret, w_branch, layer, tm=512, tn=512):
    t = y_ssd.shape[0]
    gl = OFF_GL // tn
    nj = D // tn
    return pl.pallas_call(
        _merge_kernel,
        grid=(t // tm, nj),
        in_specs=[pl.BlockSpec((tm, D), lambda i, j: (i, 0)),
                  pl.BlockSpec((tm, D), lambda i, j: (i, OFF_Z // D)),
                  pl.BlockSpec((None, 1, D), lambda i, j: (layer, 0, 0)),
                  pl.BlockSpec((tm, D), lambda i, j: (i, 0)),
                  pl.BlockSpec((tm, D), lambda i, j: (i, 0)),
                  pl.BlockSpec((tm, tn), lambda i, j: (i, gl + j)),
                  pl.BlockSpec((tm, tn), lambda i, j: (i, gl + nj + j)),
                  pl.BlockSpec((tm, tn), lambda i, j: (i, gl + 2 * nj + j)),
                  pl.BlockSpec((None, 3, D, tn), lambda i, j: (layer, 0, 0, j))],
        out_specs=pl.BlockSpec((tm, tn), lambda i, j: (i, j)),
        out_shape=jax.ShapeDtypeStruct((t, D), BF16),
        scratch_shapes=[pltpu.VMEM((tm, D), BF16)],
        compiler_params=_cp(("parallel", "arbitrary")),
        name="merge",
    )(y_ssd, proj, ssd_ng, y_att, y_ret, proj, proj, proj, w_branch)


def _outproj_kernel(m_ref, w_ref, x_ref, g1_ref, ng_ref, sh_ref, sc_ref, wr_ref,
                    x1_ref, h2_ref, aff_ref, afft_ref):
    x1 = x_ref[...] + g1_ref[...] * _dot(m_ref[...], w_ref[...])
    x1_ref[...] = x1
    h = _rms(x1, ng_ref[...]) * (1.0 + sc_ref[...]) + sh_ref[...]
    hb = h.astype(BF16)
    h2_ref[...] = hb
    hl = (h - hb.astype(F32)).astype(BF16)
    logits = _dot(hb, wr_ref[0]) + _dot(hl, wr_ref[0]) + _dot(hb, wr_ref[1])
    lane = lax.broadcasted_iota(jnp.int32, logits.shape, 1)
    logits = jnp.where(lane < N_EXP, logits, -jnp.inf)
    e = jnp.exp(logits - jnp.max(logits, axis=-1, keepdims=True))
    aff = e / jnp.sum(e, axis=-1, keepdims=True)
    aff_ref[...] = aff
    afft_ref[...] = aff.T[0:N_EXP, :]


def _outproj(merged, w_out, x, mods, norm_g, w_router, layer, mod_row, tm=512):
    t = x.shape[0]

    def ms(col):
        return pl.BlockSpec((None, None, 1, D), lambda i: (layer, mod_row(i, tm), 0, col))

    return pl.pallas_call(
        _outproj_kernel,
        grid=(t // tm,),
        in_specs=[pl.BlockSpec((tm, D), lambda i: (i, 0)),
                  pl.BlockSpec((None, D, D), lambda i: (layer, 0, 0)),
                  pl.BlockSpec((tm, D), lambda i: (i, 0)),
                  ms(2),
                  pl.BlockSpec((None, 1, D), lambda i: (layer, 0, 0)),
                  ms(3), ms(4),
                  pl.BlockSpec((None, 2, D, 128), lambda i: (layer, 0, 0, 0))],
        out_specs=[pl.BlockSpec((tm, D), lambda i: (i, 0)),
                   pl.BlockSpec((tm, D), lambda i: (i, 0)),
                   pl.BlockSpec((tm, 128), lambda i: (i, 0)),
                   pl.BlockSpec((N_EXP, tm), lambda i: (0, i))],
        out_shape=[jax.ShapeDtypeStruct((t, D), F32),
                   jax.ShapeDtypeStruct((t, D), BF16),
                   jax.ShapeDtypeStruct((t, 128), F32),
                   jax.ShapeDtypeStruct((N_EXP, t), F32)],
        compiler_params=_cp(("parallel",)),
        name="out_proj",
    )(merged, w_out, x, mods, norm_g, mods, mods, w_router)


SUB = 128
WIN = SUB + 16


def _select_kernel(a_ref, post_ref, postok_ref, st_ref, cnt_ref, *, groups):
    ri = lax.broadcasted_iota(jnp.int32, (SUB, SUB), 0)
    ci = lax.broadcasted_iota(jnp.int32, (SUB, SUB), 1)
    before = jnp.where(ri < ci, 1.0, 0.0).astype(BF16)
    lane = lax.broadcasted_iota(jnp.int32, (N_EXP, SUB), 1)
    st_ref[...] = jnp.zeros_like(st_ref)
    for lo, n, cap, base in groups:
        bits = pltpu.bitcast(a_ref[:, lo:lo + n], jnp.int32)

        def search(i, prefix, bits=bits, cap=cap):
            cand = prefix | jnp.left_shift(jnp.int32(1), 30 - i)
            cnt = jnp.sum(jnp.where(bits >= cand, 1.0, 0.0), axis=1, keepdims=True)
            return jnp.where(cnt >= cap, cand, prefix)

        zero = pltpu.bitcast(jnp.minimum(jnp.min(a_ref[:, lo:lo + n], axis=1, keepdims=True), 0.0), jnp.int32)
        tau = lax.fori_loop(0, 31, search, zero)
        need = cap - jnp.sum(jnp.where(bits > tau, 1.0, 0.0), axis=1, keepdims=True)

        cnt_ref[...] = jnp.zeros_like(cnt_ref)

        def block(b, carry, lo=lo, base=base, tau=tau, need=need):
            ceq = cnt_ref[0][:, 0:1]
            csel = cnt_ref[1][:, 0:1]
            t0 = pl.multiple_of(lo + b * SUB, SUB)
            bb = pltpu.bitcast(a_ref[:, pl.ds(t0, SUB)], jnp.int32)
            eq = jnp.where(bb == tau, 1.0, 0.0)
            rank_eq = ceq + _dot(eq.astype(BF16), before)
            sel = jnp.where((bb > tau) | ((bb == tau) & (rank_eq < need)), 1.0, 0.0)
            pos = csel + _dot(sel.astype(BF16), before) + base
            pm = jnp.where(sel > 0.0, pos, -1.0)
            post_ref[:, pl.ds(t0, SUB)] = pm
            full = jnp.concatenate([pm, jnp.full((SUB - N_EXP, SUB), -1.0, F32)], axis=0)
            postok_ref[pl.ds(t0, SUB), :] = full.T
            st_ref[...] = jnp.where(lane == lo // SUB + b, (csel + base).astype(jnp.int32), st_ref[...])
            cnt_ref[0] = jnp.broadcast_to(ceq + jnp.sum(eq, axis=1, keepdims=True), (N_EXP, SUB))
            cnt_ref[1] = jnp.broadcast_to(csel + jnp.sum(sel, axis=1, keepdims=True), (N_EXP, SUB))
            return carry

        lax.fori_loop(0, n // SUB, block, 0)


def _select(afft, groups):
    t = afft.shape[1]
    assert t // SUB <= SUB
    return pl.pallas_call(
        functools.partial(_select_kernel, groups=groups),
        out_shape=[jax.ShapeDtypeStruct((N_EXP, t), F32),
                   jax.ShapeDtypeStruct((t, SUB), F32),
                   jax.ShapeDtypeStruct((N_EXP, SUB), jnp.int32)],
        scratch_shapes=[pltpu.VMEM((2, N_EXP, SUB), F32)],
        compiler_params=pltpu.CompilerParams(vmem_limit_bytes=VMEM_LIMIT),
        name="select",
    )(afft)


def _ffn_kernel(st_ref, h_ref, pos_ref, w1_ref, w3_ref, w2_ref, o_ref, stage_ref, hid_ref,
                *, n_e, nbd, nf, n2, cap, tbd, tf):
    r = pl.program_id(0)
    s = pl.program_id(1)

    @pl.when((r < n_e) & (s < nbd))
    def _():
        slot = r % 2

        @pl.when(s == 0)
        def _():
            stage_ref[slot] = jnp.zeros(stage_ref.shape[1:], BF16)

        srow = lax.broadcasted_iota(jnp.int32, (WIN, SUB), 0).astype(F32)
        for k in range(tbd // SUB):
            off16 = pl.multiple_of((st_ref[r, s * (tbd // SUB) + k] // 16) * 16, 16)
            rel = pos_ref[:, k * SUB:(k + 1) * SUB] - off16.astype(F32)
            onehot = jnp.where(srow == rel, 1.0, 0.0).astype(BF16)
            rows = _dot(onehot, h_ref[k * SUB:(k + 1) * SUB, :]).astype(BF16)
            stage_ref[slot, pl.ds(off16, WIN), :] += rows

    @pl.when((r > 0) & (s < nf))
    def _():
        x = stage_ref[(r + 1) % 2, 0:cap, :]
        hid = _silu(_dot(x, w1_ref[...].astype(BF16))) * _dot(x, w3_ref[...].astype(BF16))
        hid_ref[:, pl.ds(pl.multiple_of(s * tf, tf), tf)] = hid.astype(BF16)

    @pl.when((r > 0) & (s >= nf) & (s < nf + n2))
    def _():
        o_ref[...] = _dot(hid_ref[...], w2_ref[...].astype(BF16)).astype(BF16)


def _ffn(starts, h2, post, w1, w3, w2, layer, cap, tf=256, tn=256):
    t = h2.shape[0]
    n_e, ff = w1.shape[1], w1.shape[-1]
    nf = ff // tf
    n2 = D // tn
    tbd = max(SUB, (t // (nf + n2)) // SUB * SUB)
    while t % tbd:
        tbd -= SUB
    nbd = t // tbd
    steps = max(nbd, nf + n2)

    def tok(r, s):
        return jnp.where(r < n_e, jnp.minimum(s, nbd - 1), nbd - 1)

    def prev(r):
        return jnp.maximum(r - 1, 0)

    def fcol(s):
        return jnp.minimum(s, nf - 1)

    def ocol(r, s):
        return jnp.where(r > 0, jnp.clip(s - nf, 0, n2 - 1), 0)

    grid_spec = pltpu.PrefetchScalarGridSpec(
        num_scalar_prefetch=1,
        grid=(n_e + 1, steps),
        in_specs=[pl.BlockSpec((tbd, D), lambda r, s, st: (tok(r, s), 0)),
                  pl.BlockSpec((None, 1, tbd), lambda r, s, st: (jnp.minimum(r, n_e - 1), 0, tok(r, s))),
                  pl.BlockSpec((None, None, D, tf), lambda r, s, st: (layer, prev(r), 0, fcol(s))),
                  pl.BlockSpec((None, None, D, tf), lambda r, s, st: (layer, prev(r), 0, fcol(s))),
                  pl.BlockSpec((None, None, ff, tn), lambda r, s, st: (layer, prev(r), 0, ocol(r, s)))],
        out_specs=pl.BlockSpec((None, cap, tn), lambda r, s, st: (prev(r), 0, ocol(r, s))),
        scratch_shapes=[pltpu.VMEM((2, cap + WIN, D), BF16), pltpu.VMEM((cap, ff), BF16)],
    )
    return pl.pallas_call(
        functools.partial(_ffn_kernel, n_e=n_e, nbd=nbd, nf=nf, n2=n2, cap=cap, tbd=tbd, tf=tf),
        grid_spec=grid_spec,
        out_shape=jax.ShapeDtypeStruct((n_e, cap, D), BF16),
        compiler_params=_cp(("arbitrary", "arbitrary")),
        name="expert_ffn",
    )(starts, h2, post.reshape(n_e, 1, t), w1, w3, w2)


def _combine_kernel(st_ref, ye_hbm, postok_ref, aff_ref, x1_ref, g2_ref, o_ref, win_ref, sem, *, cap):
    b = pl.program_id(0)
    nb = pl.num_programs(0)

    def first_slot(bb, e):
        return pl.multiple_of(jnp.minimum((st_ref[e, bb] // 16) * 16, cap - WIN), 16)

    def window_copy(bb, slot, e):
        return pltpu.make_async_copy(ye_hbm.at[e, pl.ds(first_slot(bb, e), WIN), :],
                                     win_ref.at[slot, pl.ds(e * WIN, WIN), :], sem.at[slot])

    @pl.when(b == 0)
    def _():
        tail = jnp.zeros((2 * SUB - WIN, D), BF16)
        win_ref[0, N_EXP * WIN:, :] = tail
        win_ref[1, N_EXP * WIN:, :] = tail
        for e in range(N_EXP):
            window_copy(0, 0, e).start()

    @pl.when(b + 1 < nb)
    def _():
        for e in range(N_EXP):
            window_copy(b + 1, (b + 1) % 2, e).start()

    slot = b % 2
    for e in range(N_EXP):
        window_copy(b, slot, e).wait()

    lane = lax.broadcasted_iota(jnp.int32, (SUB, 2 * SUB), 1).astype(F32)
    acc = jnp.zeros((SUB, D), F32)
    for e in range(N_EXP):
        rel = postok_ref[:, e:e + 1] - first_slot(b, e).astype(F32)
        gated = jnp.where(lane == rel, aff_ref[:, e:e + 1], 0.0).astype(BF16)
        acc += _dot(gated, win_ref[slot, pl.ds(e * WIN, 2 * SUB), :])
    o_ref[...] = x1_ref[...] + g2_ref[...] * acc


def _combine(starts, ye, postok, aff, x1, mods, layer, mod_row):
    t = x1.shape[0]
    n_e, cap, _ = ye.shape
    grid_spec = pltpu.PrefetchScalarGridSpec(
        num_scalar_prefetch=1,
        grid=(t // SUB,),
        in_specs=[pl.BlockSpec(memory_space=pl.ANY),
                  pl.BlockSpec((SUB, SUB), lambda b, st: (b, 0)),
                  pl.BlockSpec((SUB, SUB), lambda b, st: (b, 0)),
                  pl.BlockSpec((SUB, D), lambda b, st: (b, 0)),
                  pl.BlockSpec((None, None, 1, D), lambda b, st: (layer, mod_row(b, SUB), 0, 5))],
        out_specs=pl.BlockSpec((SUB, D), lambda b, st: (b, 0)),
        scratch_shapes=[pltpu.VMEM((2, N_EXP * WIN + 2 * SUB - WIN, D), BF16),
                        pltpu.SemaphoreType.DMA((2,))],
    )
    return pl.pallas_call(
        functools.partial(_combine_kernel, cap=cap),
        grid_spec=grid_spec,
        out_shape=jax.ShapeDtypeStruct((t, D), F32),
        compiler_params=_cp(("arbitrary",)),
        name="combine",
    )(starts, ye, postok, aff, x1, mods)


def _rope_tables(n_tokens):
    rows_n = n_tokens // GRID_W
    row = jnp.repeat(jnp.arange(rows_n, dtype=F32), GRID_W)
    col = jnp.tile(jnp.arange(GRID_W, dtype=F32), rows_n)
    n_freq = ATT_D // 4
    inv = ROPE_THETA ** (-jnp.arange(n_freq, dtype=F32) / n_freq)
    ang_r = row[:, None] * inv
    ang_c = col[:, None] * inv
    zeros = jnp.zeros_like(ang_r)
    cos = jnp.concatenate([jnp.cos(ang_r)] * 2 + [jnp.cos(ang_c)] * 2, axis=1)
    sa = jnp.concatenate([-jnp.sin(ang_r), zeros, -jnp.sin(ang_c), zeros], axis=1)
    sb = jnp.concatenate([zeros, jnp.sin(ang_r), zeros, jnp.sin(ang_c)], axis=1)
    return cos, sa, sb


def _group_dt_lanes(v):
    depth = v.shape[0]
    v = v.reshape(depth, 2, SSD_G, SSD_R).transpose(0, 2, 1, 3).reshape(depth, SSD_G, 1, 2 * SSD_R)
    return jnp.pad(v, ((0, 0), (0, 0), (0, 0), (0, 128 - 2 * SSD_R)))


def _prep_w_in(w_in):
    depth = w_in.shape[0]
    dt = w_in[:, :, ORIG_DT:ORIG_AFTER_DT].reshape(depth, D, 2, SSD_G, SSD_R)
    dt = dt.transpose(0, 1, 3, 2, 4).reshape(depth, D, SSD_G, 2 * SSD_R)
    dt = jnp.pad(dt, ((0, 0), (0, 0), (0, 0), (0, 128 - 2 * SSD_R))).reshape(depth, D, SSD_G * 128)
    w = jnp.concatenate([w_in[:, :, :ORIG_DT], w_in[:, :, ORIG_AFTER_DT:], dt], axis=2)
    return w.astype(BF16)


def kernel(x_prompt, x_sample, cache_attn_k, cache_attn_v, state_ssd_fwd, state_ssd_bwd, state_ret_fwd, state_ret_bwd, c, c_ctx, norm1_g, norm2_g, w_ada, b_ada, w_in, ssd_conv_w, ssd_conv_b, ssd_a_log, ssd_dt_bias, ssd_d, ssd_norm_g, q_norm_g, k_norm_g, ret_decay, ret_norm_g, w_branch, w_out, w_router, w_e1, w_e3, w_e2):
    depth = w_in.shape[0]
    pb, pseq, _ = x_prompt.shape
    sb, sseq, _ = x_sample.shape
    n_p = pb * pseq
    n_s = sb * sseq
    assert n_p % 2048 == 0 and sseq % 2048 == 0 and 1 + sb <= 8

    def mod_row(i, tm):
        return jnp.where(i < n_p // tm, 0, 1 + (i * tm - n_p) // sseq)

    w_in_b = _prep_w_in(w_in)
    w_branch_b = w_branch.astype(BF16)
    w_out_b = w_out.astype(BF16)
    cap_p = CAP_FACTOR * n_p // N_EXP
    cap_s = CAP_FACTOR * n_s // N_EXP
    groups = ((0, n_p, cap_p, 0), (n_p, n_s, cap_s, cap_p))
    w_router_f = jnp.pad(w_router, ((0, 0), (0, 0), (0, 128 - N_EXP)))
    w_router_hi = w_router_f.astype(BF16)
    w_router_p = jnp.stack([w_router_hi, (w_router_f - w_router_hi.astype(F32)).astype(BF16)], axis=1)
    dtb = _group_dt_lanes(ssd_dt_bias)
    alog = _group_dt_lanes(ssd_a_log)
    dvec = jnp.repeat(ssd_d, SSD_P, axis=1).reshape(depth, 1, D)
    rd = jnp.broadcast_to(ret_decay.transpose(0, 2, 1)[..., None], (depth, RET_H, 2, 128))
    rope = _rope_tables(sseq)
    n1g = norm1_g.reshape(depth, 1, D)
    n2g = norm2_g.reshape(depth, 1, D)
    ssd_ng = ssd_norm_g.reshape(depth, 1, D)
    ret_ng = ret_norm_g.reshape(depth, 1, D)
    qg = q_norm_g.reshape(depth, 1, ATT_D)
    kg = k_norm_g.reshape(depth, 1, ATT_D)

    cond8 = jnp.zeros((8, D), F32).at[0].set(c_ctx).at[1:1 + sb].set(c)
    mods = _ada(cond8, w_ada, b_ada).reshape(depth, 8, 1, 6 * D)

    x = jnp.concatenate([x_prompt.reshape(n_p, D), x_sample.reshape(n_s, D)], axis=0)
    new_k, new_v, new_hf, new_hb, new_sf, new_sb = [], [], [], [], [], []
    y_ssd = jnp.zeros((n_p + n_s, D), F32)
    y_att = jnp.zeros((n_p + n_s, D), BF16)
    y_ret = jnp.zeros((n_p + n_s, D), BF16)
    for l in range(depth):
        proj = _inproj(x, n1g, mods, w_in_b, l, mod_row)

        ys_p, hf, hb = _ssd(proj, 0, pb, pseq, ssd_conv_w, ssd_conv_b.reshape(depth, 1, -1), dtb, alog, dvec,
                            l, out_state=True, dst=y_ssd)
        (y_ssd,) = _ssd(proj, n_p, sb, sseq, ssd_conv_w, ssd_conv_b.reshape(depth, 1, -1), dtb, alog, dvec,
                        l, h0=(state_ssd_fwd, state_ssd_bwd), dst=ys_p)
        ya_p, kn, vn = _att(proj, 0, pb, pseq, qg, kg, l, tq=pseq, dst=y_att)
        (y_att,) = _att(proj, n_p, sb, sseq, qg, kg, l, tq=512, ctx=(cache_attn_k, cache_attn_v), rope=rope,
                        dst=ya_p)
        yr_p, sf, sbk = _ret(proj, 0, pb, pseq, rd, ret_ng, l, out_state=True, dst=y_ret)
        (y_ret,) = _ret(proj, n_p, sb, sseq, rd, ret_ng, l, rope=rope, s0=(state_ret_fwd, state_ret_bwd),
                        dst=yr_p)
        new_k.append(kn), new_v.append(vn), new_hf.append(hf), new_hb.append(hb)
        new_sf.append(sf), new_sb.append(sbk)

        merged = _merge(y_ssd, proj, ssd_ng, y_att, y_ret, w_branch_b, l)
        x1, h2, aff, afft = _outproj(merged, w_out_b, x, mods, n2g, w_router_p, l, mod_row)
        post, postok, starts = _select(afft, groups)
        ye = _ffn(starts, h2, post, w_e1, w_e3, w_e2, l, cap_p + cap_s)
        x = _combine(starts, ye, postok, aff, x1, mods, l, mod_row)

    y_prompt = x[:n_p].reshape(pb, pseq, D)
    y_sample = x[n_p:].reshape(sb, sseq, D)
    return (y_prompt, y_sample, jnp.stack(new_k, axis=1), jnp.stack(new_v, axis=1),
            jnp.stack(new_hf, axis=1), jnp.stack(new_hb, axis=1),
            jnp.stack(new_sf, axis=1), jnp.stack(new_sb, axis=1))
```
